```python
import math
import jax
import jax.numpy as jnp
from jax import lax
import numpy as np

D_MODEL = 1024
BATCH = 8
SEQ = 4096
DEPTH = 2
DEC_BATCH = 32
DEC_SEQ = 4
PAST_LEN = 16384
PAGE_SIZE = 128

S5_CH = 16
S5_WIDTH = D_MODEL // 4
S5_GROUPS = S5_WIDTH // S5_CH
S5_STATE = 64
RET_HEAD_DIM = 64
RET_WIDTH = D_MODEL // 4
RET_HEADS = RET_WIDTH // RET_HEAD_DIM
RET_CHUNK = 128
ROPE_BASE = 10000.0
SB_HEAD_DIM = 64
SB_WIDTH = D_MODEL - S5_WIDTH - RET_WIDTH
SB_HEADS = SB_WIDTH // SB_HEAD_DIM
SB_BLOCK = 128
SB_BIAS_INIT = -6.0
IN_WIDTH = S5_WIDTH + 4 * RET_WIDTH + 3 * SB_WIDTH
D_FF = 128 * ((8 * D_MODEL // 3 + 127) // 128)
CONV_W = 3
PLE_DIM = 256
EPS = 1e-6

kernel_name = 'hymba_s5_retention_stickbreaking_step'


def rmsnorm(x, gain):
    xf = x.astype(jnp.float32)
    xf = xf * lax.rsqrt(jnp.mean(xf * xf, axis=-1, keepdims=True) + EPS)
    return xf.astype(x.dtype) * gain


def head_groupnorm(x, gain):
    xf = x.astype(jnp.float32)
    xc = xf - jnp.mean(xf, axis=-1, keepdims=True)
    xc = xc * lax.rsqrt(jnp.mean(xc * xc, axis=-1, keepdims=True) + EPS)
    return xc.astype(x.dtype) * gain


def rope(x, pos):
    half = x.shape[-1] // 2
    inv = jnp.power(ROPE_BASE, -jnp.arange(half, dtype=jnp.float32) / half)
    ang = pos.astype(jnp.float32)[:, None] * inv[None, :]
    cos = jnp.cos(ang)[None, :, None, :]
    sin = jnp.sin(ang)[None, :, None, :]
    xf = x.astype(jnp.float32)
    x1, x2 = xf[..., :half], xf[..., half:]
    out = jnp.concatenate([x1 * cos - x2 * sin, x2 * cos + x1 * sin], axis=-1)
    return out.astype(x.dtype)


def _cplx_affine_combine(e1, e2):
    a1r, a1i, b1r, b1i = e1
    a2r, a2i, b2r, b2i = e2
    return (a2r * a1r - a2i * a1i,
            a2r * a1i + a2i * a1r,
            a2r * b1r - a2i * b1i + b2r,
            a2r * b1i + a2i * b1r + b2i)


def s5_mixer(u, lam_re, lam_im, log_dt, b_re, b_im, c_re, c_im, d_skip, glu_w, glu_b, state0):
    bsz, seq, _ = u.shape
    uf = u.astype(jnp.float32).reshape(bsz, seq, S5_GROUPS, S5_CH)
    dt = jnp.exp(log_dt.astype(jnp.float32))[:, None]
    lr = lam_re.astype(jnp.float32)
    li = lam_im.astype(jnp.float32)
    mag = jnp.exp(lr * dt)
    ang = li * dt
    ab_re = mag * jnp.cos(ang)
    ab_im = mag * jnp.sin(ang)
    den = lr * lr + li * li
    f_re = ((ab_re - 1.0) * lr + ab_im * li) / den
    f_im = (ab_im * lr - (ab_re - 1.0) * li) / den
    br = b_re.astype(jnp.float32)
    bi = b_im.astype(jnp.float32)
    bb_re = f_re[..., None] * br - f_im[..., None] * bi
    bb_im = f_re[..., None] * bi + f_im[..., None] * br
    drv_re = jnp.einsum('blgc,gnc->blgn', uf, bb_re)
    drv_im = jnp.einsum('blgc,gnc->blgn', uf, bb_im)
    if state0 is not None:
        s_re = state0[0].astype(jnp.float32)
        s_im = state0[1].astype(jnp.float32)
        drv_re = drv_re.at[:, 0].add(ab_re * s_re - ab_im * s_im)
        drv_im = drv_im.at[:, 0].add(ab_re * s_im + ab_im * s_re)
    a_re = jnp.broadcast_to(ab_re, drv_re.shape)
    a_im = jnp.broadcast_to(ab_im, drv_im.shape)
    _, _, h_re, h_im = lax.associative_scan(_cplx_affine_combine, (a_re, a_im, drv_re, drv_im), axis=1)
    y = (jnp.einsum('blgn,gcn->blgc', h_re, c_re.astype(jnp.float32))
         - jnp.einsum('blgn,gcn->blgc', h_im, c_im.astype(jnp.float32))
         + d_skip.astype(jnp.float32).reshape(S5_GROUPS, S5_CH) * uf)
    y = y.reshape(bsz, seq, S5_WIDTH)
    gy = jax.nn.gelu(y)
    out = gy * jax.nn.sigmoid(gy @ glu_w.astype(jnp.float32) + glu_b.astype(jnp.float32))
    return out.astype(u.dtype), h_re[:, -1], h_im[:, -1]


def retention(q, k, v, gate, gn_gain, state0):
    bsz, seq, nh, hd = q.shape
    chunk = min(RET_CHUNK, seq)
    n_chunks = seq // chunk
    log_g = jnp.log1p(-jnp.exp2(-5.0 - jnp.arange(nh, dtype=jnp.float32)))
    idx = jnp.arange(chunk, dtype=jnp.float32)
    diff = idx[:, None] - idx[None, :]
    causal = diff >= 0
    dmask = jnp.where(causal[None], jnp.exp(jnp.where(causal, diff, 0.0)[None] * log_g[:, None, None]), 0.0)
    xi = jnp.exp((idx + 1.0)[None, :] * log_g[:, None])
    zeta = jnp.exp((chunk - 1.0 - idx)[None, :] * log_g[:, None])
    d_chunk = jnp.exp(chunk * log_g)

    def to_chunks(t):
        return t.astype(jnp.float32).reshape(bsz, n_chunks, chunk, nh, hd).transpose(1, 0, 3, 2, 4)

    def step(s, inp):
        qc, kc, vc = inp
        scores = jnp.einsum('bhid,bhjd->bhij', qc, kc) * dmask
        o = (jnp.einsum('bhij,bhjd->bhid', scores, vc)
             + jnp.einsum('bhid,bhde->bhie', qc, s) * xi[None, :, :, None])
        s = s * d_chunk[None, :, None, None] + jnp.einsum('bhjd,bhje->bhde', kc * zeta[None, :, :, None], vc)
        return s, o

    s0 = jnp.zeros((bsz, nh, hd, hd), jnp.float32) if state0 is None else state0.astype(jnp.float32)
    s_fin, o = lax.scan(step, s0, (to_chunks(q), to_chunks(k), to_chunks(v)))
    o = o.transpose(1, 0, 3, 2, 4).reshape(bsz, seq, nh, hd)
    o = head_groupnorm(o, gn_gain.reshape(nh, hd)).reshape(bsz, seq, nh * hd)
    return (jax.nn.silu(gate) * o.astype(gate.dtype)), s_fin


def stick_breaking_block(qb, k, v, bias, q_pos, k_pos):
    z = (jnp.einsum('bqhd,bkhd->bhqk', qb, k).astype(jnp.float32) * (SB_HEAD_DIM ** -0.5)
         + bias.astype(jnp.float32)[None, :, None, None])
    causal = (k_pos[None, :] < q_pos[:, None])[None, None]
    log_fail = jnp.where(causal, jax.nn.log_sigmoid(-z), 0.0)
    after = lax.cumsum(log_fail, axis=3, reverse=True) - log_fail
    w = jnp.where(causal, jnp.exp(jax.nn.log_sigmoid(z) + after), 0.0)
    return jnp.einsum('bhqk,bkhd->bqhd', w.astype(v.dtype), v)


def stick_breaking(q, k, v, bias, pos0):
    bsz, seq, nh, hd = q.shape
    blk = min(SB_BLOCK, seq)
    n_blk = seq // blk
    k_pos = jnp.arange(k.shape[1], dtype=jnp.int32)

    def one_block(bi):
        qb = lax.dynamic_slice_in_dim(q, bi * blk, blk, axis=1)
        q_pos = pos0 + bi * blk + jnp.arange(blk, dtype=jnp.int32)
        return stick_breaking_block(qb, k, v, bias, q_pos, k_pos)

    out = lax.map(one_block, jnp.arange(n_blk, dtype=jnp.int32))
    return out.transpose(1, 0, 2, 3, 4).reshape(bsz, seq, nh, hd)


def conv_ffn(x, w_up, w_gate, conv_w, conv_b, w_down, buf):
    seq = x.shape[1]
    u = x @ w_up
    g = x @ w_gate
    if buf is None:
        buf = jnp.zeros((x.shape[0], CONV_W - 1, u.shape[-1]), u.dtype)
    ext = jnp.concatenate([buf.astype(u.dtype), u], axis=1)
    c = conv_b + conv_w[0] * ext[:, 0:seq]
    for j in range(1, CONV_W):
        c = c + conv_w[j] * ext[:, j:j + seq]
    out = (jax.nn.gelu(c) * g) @ w_down
    return out, ext[:, seq:]


def hybrid_layer(h, p_l, W, past, pos0):
    bsz, seq, _ = h.shape
    pos = pos0 + jnp.arange(seq, dtype=jnp.int32)
    n = rmsnorm(h, W['norm_mix'])
    z = n @ W['w_in']
    cuts = [S5_WIDTH]
    for wdt in [RET_WIDTH] * 4 + [SB_WIDTH] * 2:
        cuts.append(cuts[-1] + wdt)
    u_a, q_r, k_r, v_r, g_r, q_s, k_s, v_s = jnp.split(z, cuts, axis=-1)

    y_a, ssm_re, ssm_im = s5_mixer(u_a, W['lam_re'], W['lam_im'], W['log_dt'], W['b_re'], W['b_im'],
                                   W['c_re'], W['c_im'], W['d_skip'], W['glu_w'], W['glu_b'],
                                   None if past is None else past['ssm'])

    q_r = rope(q_r.reshape(bsz, seq, RET_HEADS, RET_HEAD_DIM), pos)
    k_r = rope(k_r.reshape(bsz, seq, RET_HEADS, RET_HEAD_DIM), pos) * (RET_HEAD_DIM ** -0.5)
    v_r = v_r.reshape(bsz, seq, RET_HEADS, RET_HEAD_DIM)
    y_r, ret_state = retention(q_r, k_r, v_r, g_r, W['ret_gn'], None if past is None else past['ret'])

    q_s = rmsnorm(q_s.reshape(bsz, seq, SB_HEADS, SB_HEAD_DIM), W['q_gain'])
    k_s = rmsnorm(k_s.reshape(bsz, seq, SB_HEADS, SB_HEAD_DIM), W['k_gain'])
    v_s = v_s.reshape(bsz, seq, SB_HEADS, SB_HEAD_DIM)
    if past is None:
        k_all, v_all = k_s, v_s
    else:
        k_all = jnp.concatenate([past['k'].astype(k_s.dtype), k_s], axis=1)
        v_all = jnp.concatenate([past['v'].astype(v_s.dtype), v_s], axis=1)
    y_s = stick_breaking(q_s, k_all, v_all, W['sb_bias'], pos0).reshape(bsz, seq, SB_WIDTH)

    mg = W['merge']
    y = jnp.concatenate([rmsnorm(y_a, mg[:S5_WIDTH]),
                         rmsnorm(y_r, mg[S5_WIDTH:S5_WIDTH + RET_WIDTH]),
                         rmsnorm(y_s, mg[S5_WIDTH + RET_WIDTH:])], axis=-1)
    h = h + y @ W['w_out']

    f, conv_state = conv_ffn(rmsnorm(h, W['norm_ffn']), W['w_up'], W['w_gate'], W['conv_w'], W['conv_b'],
                             W['w_down'], None if past is None else past['conv'])
    h = h + f

    gate = jax.nn.sigmoid(rmsnorm(h, W['ple_norm']) @ W['ple_gate'])
    h = h + (p_l @ W['ple_proj']) * gate
    return h, (k_s, v_s, ssm_re, ssm_im, ret_state, conv_state)


def setup_inputs(seed: int = 0) -> dict:
    key = jax.random.key(seed)
    ks = iter(jax.random.split(key, 48))
    f32 = jnp.float32

    def nrm(shape, scale):
        return scale * jax.random.normal(next(ks), shape, f32)

    def gain(shape):
        return 1.0 + 0.05 * jax.random.normal(next(ks), shape, f32)

    n_pages = PAST_LEN // PAGE_SIZE
    n_pool = (DEC_BATCH * n_pages * 5) // 4

    x_prompt = nrm((BATCH, SEQ, D_MODEL), 1.0)
    x_sample = nrm((DEC_BATCH, DEC_SEQ, D_MODEL), 1.0)
    cache_k = nrm((DEPTH, n_pool, PAGE_SIZE, SB_HEADS, SB_HEAD_DIM), 1.0)
    cache_v = nrm((DEPTH, n_pool, PAGE_SIZE, SB_HEADS, SB_HEAD_DIM), 1.0)
    state_ssm_re = nrm((DEPTH, DEC_BATCH, S5_GROUPS, S5_STATE), 0.1)
    state_ssm_im = nrm((DEPTH, DEC_BATCH, S5_GROUPS, S5_STATE), 0.1)
    state_ret = nrm((DEPTH, DEC_BATCH, RET_HEADS, RET_HEAD_DIM, RET_HEAD_DIM), 0.5)
    state_conv = nrm((DEPTH, DEC_BATCH, CONV_W - 1, D_FF), 1.0)
    perm = jax.random.permutation(next(ks), n_pool)
    page_table = perm[:DEC_BATCH * n_pages].reshape(DEC_BATCH, n_pages).astype(jnp.int32)
    p_prompt = nrm((DEPTH, BATCH, SEQ, PLE_DIM), 1.0)
    p_sample = nrm((DEPTH, DEC_BATCH, DEC_SEQ, PLE_DIM), 1.0)

    norm_mix_gain = gain((DEPTH, D_MODEL))
    w_in = nrm((DEPTH, D_MODEL, IN_WIDTH), D_MODEL ** -0.5)
    s5_lambda_re = -0.5 + 0.01 * jax.random.normal(next(ks), (DEPTH, S5_GROUPS, S5_STATE), f32)
    s5_lambda_im = (math.pi * jnp.arange(S5_STATE, dtype=f32))[None, None, :] + 0.01 * jax.random.normal(
        next(ks), (DEPTH, S5_GROUPS, S5_STATE), f32)
    s5_log_dt = jax.random.uniform(next(ks), (DEPTH, S5_GROUPS), f32, math.log(1e-3), math.log(1e-1))
    s5_b_re = nrm((DEPTH, S5_GROUPS, S5_STATE, S5_CH), (2.0 * S5_CH) ** -0.5)
    s5_b_im = nrm((DEPTH, S5_GROUPS, S5_STATE, S5_CH), (2.0 * S5_CH) ** -0.5)
    s5_c_re = nrm((DEPTH, S5_GROUPS, S5_CH, S5_STATE), (2.0 * S5_STATE) ** -0.5)
    s5_c_im = nrm((DEPTH, S5_GROUPS, S5_CH, S5_STATE), (2.0 * S5_STATE) ** -0.5)
    s5_d = nrm((DEPTH, S5_WIDTH), 1.0)
    s5_glu_w = nrm((DEPTH, S5_WIDTH, S5_WIDTH), S5_WIDTH ** -0.5)
    s5_glu_b = nrm((DEPTH, S5_WIDTH), 0.02)
    ret_gn_gain = gain((DEPTH, RET_WIDTH))
    sb_q_gain = gain((DEPTH, SB_HEAD_DIM))
    sb_k_gain = gain((DEPTH, SB_HEAD_DIM))
    sb_logit_bias = SB_BIAS_INIT + 0.1 * jax.random.normal(next(ks), (DEPTH, SB_HEADS), f32)
    merge_gain = gain((DEPTH, D_MODEL))
    w_out = nrm((DEPTH, D_MODEL, D_MODEL), D_MODEL ** -0.5)
    norm_ffn_gain = gain((DEPTH, D_MODEL))
    ffn_w_up = nrm((DEPTH, D_MODEL, D_FF), D_MODEL ** -0.5)
    ffn_w_gate = nrm((DEPTH, D_MODEL, D_FF), D_MODEL ** -0.5)
    ffn_conv_w = nrm((DEPTH, CONV_W, D_FF), 0.5)
    ffn_conv_b = nrm((DEPTH, D_FF), 0.02)
    ffn_w_down = nrm((DEPTH, D_FF, D_MODEL), D_FF ** -0.5)
    ple_norm_gain = gain((DEPTH, D_MODEL))
    ple_w_gate = nrm((DEPTH, D_MODEL, D_MODEL), D_MODEL ** -0.5)
    ple_w_proj = nrm((DEPTH, PLE_DIM, D_MODEL), PLE_DIM ** -0.5)
    return {
        'x_prompt': x_prompt, 'x_sample': x_sample,
        'cache_k': cache_k, 'cache_v': cache_v,
        'state_ssm_re': state_ssm_re, 'state_ssm_im': state_ssm_im,
        'state_ret': state_ret, 'state_conv': state_conv,
        'page_table': page_table,
        'p_prompt': p_prompt, 'p_sample': p_sample,
        'norm_mix_gain': norm_mix_gain, 'w_in': w_in,
        's5_lambda_re': s5_lambda_re, 's5_lambda_im': s5_lambda_im, 's5_log_dt': s5_log_dt,
        's5_b_re': s5_b_re, 's5_b_im': s5_b_im, 's5_c_re': s5_c_re, 's5_c_im': s5_c_im,
        's5_d': s5_d, 's5_glu_w': s5_glu_w, 's5_glu_b': s5_glu_b,
        'ret_gn_gain': ret_gn_gain, 'sb_q_gain': sb_q_gain, 'sb_k_gain': sb_k_gain,
        'sb_logit_bias': sb_logit_bias,
        'merge_gain': merge_gain, 'w_out': w_out, 'norm_ffn_gain': norm_ffn_gain,
        'ffn_w_up': ffn_w_up, 'ffn_w_gate': ffn_w_gate, 'ffn_conv_w': ffn_conv_w,
        'ffn_conv_b': ffn_conv_b, 'ffn_w_down': ffn_w_down,
        'ple_norm_gain': ple_norm_gain, 'ple_w_gate': ple_w_gate, 'ple_w_proj': ple_w_proj,
    }


def reference(x_prompt, x_sample, cache_k, cache_v, state_ssm_re, state_ssm_im, state_ret, state_conv,
              page_table, p_prompt, p_sample, norm_mix_gain, w_in, s5_lambda_re, s5_lambda_im, s5_log_dt,
              s5_b_re, s5_b_im, s5_c_re, s5_c_im, s5_d, s5_glu_w, s5_glu_b, ret_gn_gain, sb_q_gain,
              sb_k_gain, sb_logit_bias, merge_gain, w_out, norm_ffn_gain, ffn_w_up, ffn_w_gate, ffn_conv_w,
              ffn_conv_b, ffn_w_down, ple_norm_gain, ple_w_gate, ple_w_proj):
    h_p = x_prompt
    h_s = x_sample
    b_s = page_table.shape[0]
    outs_p = []
    outs_s = []
    for i in range(DEPTH):
        W = dict(norm_mix=norm_mix_gain[i], w_in=w_in[i], lam_re=s5_lambda_re[i], lam_im=s5_lambda_im[i],
                 log_dt=s5_log_dt[i], b_re=s5_b_re[i], b_im=s5_b_im[i], c_re=s5_c_re[i], c_im=s5_c_im[i],
                 d_skip=s5_d[i], glu_w=s5_glu_w[i], glu_b=s5_glu_b[i], ret_gn=ret_gn_gain[i],
                 q_gain=sb_q_gain[i], k_gain=sb_k_gain[i], sb_bias=sb_logit_bias[i], merge=merge_gain[i],
                 w_out=w_out[i], norm_ffn=norm_ffn_gain[i], w_up=ffn_w_up[i], w_gate=ffn_w_gate[i],
                 conv_w=ffn_conv_w[i], conv_b=ffn_conv_b[i], w_down=ffn_w_down[i], ple_norm=ple_norm_gain[i],
                 ple_gate=ple_w_gate[i], ple_proj=ple_w_proj[i])
        h_p, st_p = hybrid_layer(h_p, p_prompt[i], W, None, 0)
        k_past = cache_k[i][page_table].reshape(b_s, -1, SB_HEADS, SB_HEAD_DIM)
        v_past = cache_v[i][page_table].reshape(b_s, -1, SB_HEADS, SB_HEAD_DIM)
        past = dict(k=k_past, v=v_past, ssm=(state_ssm_re[i], state_ssm_im[i]), ret=state_ret[i],
                    conv=state_conv[i])
        h_s, st_s = hybrid_layer(h_s, p_sample[i], W, past, PAST_LEN)
        outs_p.append(st_p)
        outs_s.append(st_s)
    k_p, v_p, sr_p, si_p, r_p, c_p = [jnp.stack(t, axis=0) for t in zip(*outs_p)]
    k_s, v_s, sr_s, si_s, r_s, c_s = [jnp.stack(t, axis=0) for t in zip(*outs_s)]
    return (h_p, h_s, k_p, v_p, sr_p, si_p, r_p, c_p, k_s, v_s, sr_s, si_s, r_s, c_s)
```

```python
import functools
import math

import jax
import jax.numpy as jnp
from jax import lax
from jax.experimental import pallas as pl
from jax.experimental.pallas import tpu as pltpu

F32 = jnp.float32
BF16 = jnp.bfloat16

EPS = 1e-6
ROPE_BASE = 10000.0
HEAD_DIM = 64
S5_CH = 16
S5_STATE = 64
PAGE_SIZE = 128
CONV_W = 3
LANES = 128
SUBLANES = 8
FF_CHUNK = 256
VMEM_LIMIT = 56 * 1024 * 1024

NT_DIMS = (((1,), (1,)), ((), ()))


def _dot(a, b):
    return jnp.dot(a.astype(BF16), b.astype(BF16), preferred_element_type=F32)


def _dot_nt(a, b):
    return lax.dot_general(a.astype(BF16), b.astype(BF16), NT_DIMS, preferred_element_type=F32)


def _dot_split(x, m):
    hi = x.astype(BF16)
    lo = (x - hi.astype(F32)).astype(BF16)
    return (jnp.dot(hi, m, preferred_element_type=F32)
            + jnp.dot(lo, m, preferred_element_type=F32))


def _rms(x, gain):
    return x * lax.rsqrt(jnp.mean(x * x, axis=-1, keepdims=True) + EPS) * gain


def _params(sem):
    return pltpu.CompilerParams(dimension_semantics=sem, vmem_limit_bytes=VMEM_LIMIT)


def _const_spec(shape, single_buffer=True):
    n = len(shape)
    if single_buffer:
        return pl.BlockSpec(shape, lambda *_: (0,) * n, pipeline_mode=pl.Buffered(1))
    return pl.BlockSpec(shape, lambda *_: (0,) * n)


def _proj_in_kernel(h_ref, g_ref, w_ref, cos_ref, sin_ref, qg_ref, kg_ref, gm_ref,
                    ua_ref, qr_ref, kr_ref, vr_ref, gr_ref, qs_ref, ks_ref, vs_ref,
                    *, wa, wr, ws):
    n = _rms(h_ref[...], g_ref[...]).astype(BF16)

    def seg(lo, width):
        return jnp.dot(n, w_ref[:, lo:lo + width], preferred_element_type=F32)

    cos = cos_ref[...]
    sin = sin_ref[...]
    lane = lax.broadcasted_iota(jnp.int32, (1, wr), 1)
    first_half = (lane & (HEAD_DIM - 1)) < HEAD_DIM // 2

    def rope(v):
        swapped = jnp.where(first_half, pltpu.roll(v, wr - HEAD_DIM // 2, 1),
                            pltpu.roll(v, HEAD_DIM // 2, 1))
        return v * cos + swapped * sin

    gm = gm_ref[...]

    def head_rms(v, gain):
        ms = jnp.dot((v * v).astype(BF16), gm, preferred_element_type=F32)
        return v * lax.rsqrt(ms + EPS) * gain

    o = 0
    ua_ref[...] = seg(o, wa)
    o += wa
    qr_ref[...] = rope(seg(o, wr))
    o += wr
    kr_ref[...] = rope(seg(o, wr)) * (HEAD_DIM ** -0.5)
    o += wr
    vr_ref[...] = seg(o, wr)
    o += wr
    gr_ref[...] = seg(o, wr)
    o += wr
    qs_ref[...] = head_rms(seg(o, ws), qg_ref[...]) * (HEAD_DIM ** -0.5)
    o += ws
    ks_ref[...] = head_rms(seg(o, ws), kg_ref[...])
    o += ws
    vs_ref[...] = seg(o, ws)


def _proj_in(h2, gain, w_in, cos_t, sin_t, q_gain, k_gain, gm, *, nb, seq, tm, wa, wr, ws):
    d = h2.shape[1]
    nt = seq // tm
    row = lambda b, t: (b * nt + t, 0)
    tspec = lambda w: pl.BlockSpec((tm, w), row)
    out_shape = [jax.ShapeDtypeStruct((seq, nb * wa), F32)]
    out_shape += [jax.ShapeDtypeStruct((nb * seq, wr), F32)] * 4
    out_shape += [jax.ShapeDtypeStruct((nb * seq, ws), F32)] * 3
    return pl.pallas_call(
        functools.partial(_proj_in_kernel, wa=wa, wr=wr, ws=ws),
        grid=(nb, nt),
        in_specs=[tspec(d), _const_spec((1, d)), _const_spec(w_in.shape),
                  pl.BlockSpec((tm, wr), lambda b, t: (t, 0)),
                  pl.BlockSpec((tm, wr), lambda b, t: (t, 0)),
                  _const_spec((1, ws)), _const_spec((1, ws)), _const_spec((ws, ws))],
        out_specs=[pl.BlockSpec((tm, wa), lambda b, t: (t, b))]
        + [tspec(wr)] * 4 + [tspec(ws)] * 3,
        out_shape=out_shape,
        compiler_params=_params(("arbitrary", "arbitrary")),
        name="proj_in",
    )(h2, gain, w_in, cos_t, sin_t, q_gain, k_gain, gm)


def _s5_kernel(u_ref, s0re_ref, s0im_ref, are_ref, aim_ref, bblk_ref, cblk_ref, dsk_ref,
               gw_ref, gb_ref, y_ref, fre_ref, fim_ref, drv_scr, hre_scr, him_scr, *, nb, tc):
    ns = hre_scr.shape[1]

    @pl.when(pl.program_id(0) == 0)
    def _():
        hre_scr[...] = s0re_ref[...]
        him_scr[...] = s0im_ref[...]

    u = u_ref[...]
    drv_scr[...] = _dot(u, bblk_ref[...])
    are = jnp.broadcast_to(are_ref[...], (nb, ns))
    aim = jnp.broadcast_to(aim_ref[...], (nb, ns))

    def step(t, carry):
        hre, him = carry
        rows = pl.ds(pl.multiple_of(t * nb, nb), nb)
        nre = are * hre - aim * him + drv_scr[rows, 0:ns]
        nim = are * him + aim * hre + drv_scr[rows, ns:2 * ns]
        drv_scr[rows, 0:ns] = nre
        drv_scr[rows, ns:2 * ns] = nim
        return nre, nim

    hre, him = lax.fori_loop(0, tc, step, (hre_scr[...], him_scr[...]))
    hre_scr[...] = hre
    him_scr[...] = him
    fre_ref[...] = hre
    fim_ref[...] = him
    y = _dot(drv_scr[...], cblk_ref[...]) + dsk_ref[...] * u
    gy = jax.nn.gelu(y)
    y_ref[...] = gy * jax.nn.sigmoid(_dot(gy, gw_ref[...]) + gb_ref[...])


def _s5(u_tm, s0re, s0im, are, aim, bblk, cblk, dsk, gw, gb, *, nb, seq, tc):
    wa = u_tm.shape[1]
    ns = are.shape[1]
    rows = tc * nb
    return pl.pallas_call(
        functools.partial(_s5_kernel, nb=nb, tc=tc),
        grid=(seq // tc,),
        in_specs=[pl.BlockSpec((rows, wa), lambda i: (i, 0)),
                  _const_spec((nb, ns)), _const_spec((nb, ns)),
                  _const_spec((1, ns)), _const_spec((1, ns)),
                  _const_spec(bblk.shape), _const_spec(cblk.shape), _const_spec((1, wa)),
                  _const_spec(gw.shape), _const_spec((1, wa))],
        out_specs=[pl.BlockSpec((rows, wa), lambda i: (i, 0)),
                   pl.BlockSpec((nb, ns), lambda i: (0, 0)),
                   pl.BlockSpec((nb, ns), lambda i: (0, 0))],
        out_shape=[jax.ShapeDtypeStruct((seq * nb, wa), F32),
                   jax.ShapeDtypeStruct((nb, ns), F32),
                   jax.ShapeDtypeStruct((nb, ns), F32)],
        scratch_shapes=[pltpu.VMEM((rows, 2 * ns), F32), pltpu.VMEM((nb, ns), F32),
                        pltpu.VMEM((nb, ns), F32)],
        compiler_params=_params(("arbitrary",)),
        name="s5",
    )(u_tm, s0re, s0im, are, aim, bblk, cblk, dsk, gw, gb)


def _head_lane_mask(width, h):
    lane = lax.broadcasted_iota(jnp.int32, (1, width), 1)
    return (lane >> 6) == h


def _ret_intra(q, k, v, dmask_ref, nh):
    wr = q.shape[1]
    o = jnp.zeros(q.shape, F32)
    for h in range(nh):
        mh = _head_lane_mask(wr, h)
        s = _dot_nt(jnp.where(mh, q, 0.0), k) * dmask_ref[h]
        o = jnp.where(mh, _dot(s, v), o)
    return o


def _ret_finish(o, gate, gn_ref, gm_ref):
    gm = gm_ref[...]
    xc = o - _dot_split(o, gm)
    var = _dot_split(xc * xc, gm)
    return jax.nn.silu(gate) * (xc * lax.rsqrt(var + EPS) * gn_ref[...])


def _ret_kernel(q_ref, k_ref, v_ref, g_ref, dmask_ref, xi_ref, zeta_ref, dcol_ref, gn_ref, gm_ref,
                y_ref, sfin_ref, s_scr, *, nh):
    wr = q_ref.shape[1]

    @pl.when(pl.program_id(1) == 0)
    def _():
        s_scr[...] = jnp.zeros(s_scr.shape, F32)

    q = q_ref[...]
    k = k_ref[...]
    v = v_ref[...]
    s_prev = s_scr[...]
    o = _ret_intra(q, k, v, dmask_ref, nh) + _dot(q * xi_ref[...], s_prev)
    row_h = lax.broadcasted_iota(jnp.int32, (wr, wr), 0) >> 6
    col_h = lax.broadcasted_iota(jnp.int32, (wr, wr), 1) >> 6
    kv = _dot((k * zeta_ref[...]).T, v)
    s_new = s_prev * dcol_ref[...] + jnp.where(row_h == col_h, kv, 0.0)
    s_scr[...] = s_new
    sfin_ref[...] = s_new
    y_ref[...] = _ret_finish(o, g_ref[...], gn_ref, gm_ref)


def _retention(q, k, v, g, dmask, xi, zeta, dcol, gn, gm, *, nb, seq, chunk, nh):
    wr = q.shape[1]
    nc = seq // chunk
    tspec = pl.BlockSpec((chunk, wr), lambda b, c: (b * nc + c, 0))
    return pl.pallas_call(
        functools.partial(_ret_kernel, nh=nh),
        grid=(nb, nc),
        in_specs=[tspec] * 4 + [_const_spec(dmask.shape), _const_spec(xi.shape),
                                _const_spec(zeta.shape), _const_spec((1, wr)),
                                _const_spec((1, wr)), _const_spec((wr, wr))],
        out_specs=[tspec, pl.BlockSpec((None, wr, wr), lambda b, c: (b, 0, 0))],
        out_shape=[jax.ShapeDtypeStruct((nb * seq, wr), F32),
                   jax.ShapeDtypeStruct((nb, wr, wr), F32)],
        scratch_shapes=[pltpu.VMEM((wr, wr), F32)],
        compiler_params=_params(("arbitrary", "arbitrary")),
        name="retention",
    )(q, k, v, g, dmask, xi, zeta, dcol, gn, gm)


def _ret_step_kernel(q_ref, k_ref, v_ref, g_ref, dmask_ref, xi_ref, zeta_ref, dcol_ref, gn_ref,
                     gm_ref, s0_ref, exp_ref, y_ref, snew_ref, *, nh, ntok):
    t_rows, wr = q_ref.shape
    sw = s0_ref.shape[0]
    q = q_ref[...]
    k = k_ref[...]
    v = v_ref[...]
    s0 = s0_ref[...]
    o = _ret_intra(q, k, v, dmask_ref, nh)
    qx = q * xi_ref[...]
    kz = k * zeta_ref[...]
    own_b = ((lax.broadcasted_iota(jnp.int32, (t_rows, sw), 1) >> 6)
             == lax.broadcasted_iota(jnp.int32, (t_rows, sw), 0) // ntok)
    s_new = s0 * dcol_ref[...]
    for h in range(nh):
        mh = _head_lane_mask(wr, h)
        q_exp = jnp.where(own_b, _dot(jnp.where(mh, qx, 0.0), exp_ref[h]), 0.0)
        k_exp = jnp.where(own_b, _dot(jnp.where(mh, kz, 0.0), exp_ref[h]), 0.0)
        o = o + jnp.where(mh, _dot(q_exp, s0), 0.0)
        s_new = s_new + jnp.where(mh, _dot(k_exp.T, v), 0.0)
    snew_ref[...] = s_new
    y_ref[...] = _ret_finish(o, g_ref[...], gn_ref, gm_ref)


def _retention_step(q, k, v, g, dmask, xi, zeta, dcol, gn, gm, s0cat, expand, *, nh, ntok):
    t_rows, wr = q.shape
    args = (q, k, v, g, dmask, xi, zeta, dcol, gn, gm, s0cat, expand)
    return pl.pallas_call(
        functools.partial(_ret_step_kernel, nh=nh, ntok=ntok),
        grid=(1,),
        in_specs=[_const_spec(a.shape) for a in args],
        out_specs=[_const_spec((t_rows, wr), False), _const_spec(s0cat.shape, False)],
        out_shape=[jax.ShapeDtypeStruct((t_rows, wr), F32),
                   jax.ShapeDtypeStruct(s0cat.shape, F32)],
        compiler_params=_params(("arbitrary",)),
        name="retention_step",
    )(*args)


def _sb_tile(qh, kt, vt, bias, umat, acc, run, causal, keys_on_lanes=False):
    z = (_dot(qh, kt) if keys_on_lanes else _dot_nt(qh, kt)) + bias
    soft = jnp.log1p(jnp.exp(-jnp.abs(z)))
    log_fail = -(jnp.maximum(z, 0.0) + soft)
    if causal is not None:
        log_fail = jnp.where(causal, log_fail, 0.0)
    after = _dot_split(log_fail, umat) + run
    w = jnp.exp(jnp.minimum(z, 0.0) - soft + after)
    if causal is not None:
        w = jnp.where(causal, w, 0.0)
    acc = acc + (_dot_nt(w, vt) if keys_on_lanes else _dot(w, vt))
    run = run + jnp.sum(log_fail, axis=1, keepdims=True)
    return acc, run


def _sb_kernel(bias_ref, q_ref, k_ref, v_ref, u_ref, o_ref, *, tq):
    hp = pl.program_id(1)
    qi = pl.program_id(2)
    q = q_ref[...]
    width = q.shape[1]
    umat = u_ref[...]
    causal = (lax.broadcasted_iota(jnp.int32, (tq, tq), 1)
              < lax.broadcasted_iota(jnp.int32, (tq, tq), 0))
    out = jnp.zeros((tq, width), F32)
    for hh in range(width // HEAD_DIM):
        mh = _head_lane_mask(width, hh)
        qh = jnp.where(mh, q, 0.0).astype(BF16)
        bias = bias_ref[hp * (width // HEAD_DIM) + hh]

        def tile(j, carry, causal_mask):
            rows = pl.ds(pl.multiple_of(j * tq, tq), tq)
            return _sb_tile(qh, k_ref[rows, :], v_ref[rows, :], bias, umat, carry[0], carry[1],
                            causal_mask)

        carry = (jnp.zeros((tq, width), F32), jnp.zeros((tq, 1), F32))
        carry = tile(qi, carry, causal)
        acc, _ = lax.fori_loop(0, qi, lambda n, c: tile(qi - 1 - n, c, None), carry)
        out = jnp.where(mh, acc, out)
    o_ref[...] = out


def _stick_breaking(q, k, v, bias, umat, *, nb, seq, tq):
    ws = q.shape[1]
    nq = seq // tq
    qspec = pl.BlockSpec((tq, LANES), lambda b, hp, i: (b * nq + i, hp))
    kspec = pl.BlockSpec((seq, LANES), lambda b, hp, i: (b, hp))
    return pl.pallas_call(
        functools.partial(_sb_kernel, tq=tq),
        grid=(nb, ws // LANES, nq),
        in_specs=[pl.BlockSpec(memory_space=pltpu.SMEM), qspec, kspec, kspec,
                  _const_spec((tq, tq))],
        out_specs=qspec,
        out_shape=jax.ShapeDtypeStruct((nb * seq, ws), F32),
        compiler_params=_params(("arbitrary", "arbitrary", "arbitrary")),
        name="stick_breaking",
    )(bias, q, k, v, umat)


def _sb_step_kernel(pt_ref, q_ref, kn_ref, vn_ref, bias_ref, u_ref, *rest, pp, nh):
    k_pages = rest[:pp]
    v_pages = rest[pp:2 * pp]
    o_ref = rest[2 * pp]
    acc_scr, run_scr = rest[2 * pp + 1:]
    j = pl.program_id(1)
    q = q_ref[...].astype(BF16)
    rows, width = q.shape
    bias = bias_ref[...]
    umat = u_ref[...]

    def page(kt, vt, causal):
        acc, run = _sb_tile(q, kt, vt, bias, umat, acc_scr[...], run_scr[...], causal,
                            keys_on_lanes=True)
        acc_scr[...] = acc
        run_scr[...] = run

    @pl.when(j == 0)
    def _():
        acc_scr[...] = jnp.zeros(acc_scr.shape, F32)
        run_scr[...] = jnp.zeros(run_scr.shape, F32)
        row_t = lax.broadcasted_iota(jnp.int32, (rows, PAGE_SIZE), 0) // nh
        col = lax.broadcasted_iota(jnp.int32, (rows, PAGE_SIZE), 1)
        page(kn_ref[...], vn_ref[...], col < row_t)

    @pl.when(j > 0)
    def _():
        for i in reversed(range(pp)):
            page(k_pages[i][...], v_pages[i][...], None)

    @pl.when(j == pl.num_programs(1) - 1)
    def _():
        lane_h = lax.broadcasted_iota(jnp.int32, (rows, width), 1) >> 6
        row_h = lax.broadcasted_iota(jnp.int32, (rows, width), 0) % nh
        own = jnp.where(lane_h == row_h, acc_scr[...], 0.0)
        o_ref[...] = jnp.sum(own.reshape(rows // nh, nh, width), axis=1)


def _stick_breaking_step(page_table, q_exp, k_new, v_new, bias_rows, umat, cache_k, cache_v, *,
                         layer, pp, nh):
    nb, rows, width = q_exp.shape
    n_pages = page_table.shape[1]
    n_steps = n_pages // pp
    ntok = rows // nh

    def page_map(slot):
        def index(b, j, pt):
            return (layer, pt[b, n_pages - jnp.maximum(j, 1) * pp + slot], 0, 0)
        return index

    page_specs = [pl.BlockSpec((None, None, width, PAGE_SIZE), page_map(s)) for s in range(pp)]
    per_b = lambda r: pl.BlockSpec((None, r, width), lambda b, j, pt: (b, 0, 0))
    new_spec = pl.BlockSpec((None, width, PAGE_SIZE), lambda b, j, pt: (b, 0, 0))
    grid_spec = pltpu.PrefetchScalarGridSpec(
        num_scalar_prefetch=1,
        grid=(nb, n_steps + 1),
        in_specs=[per_b(rows), new_spec, new_spec,
                  pl.BlockSpec((rows, 1), lambda b, j, pt: (0, 0)),
                  pl.BlockSpec((PAGE_SIZE, PAGE_SIZE), lambda b, j, pt: (0, 0))]
        + page_specs + page_specs,
        out_specs=per_b(ntok),
        scratch_shapes=[pltpu.VMEM((rows, width), F32), pltpu.VMEM((rows, 1), F32)],
    )
    return pl.pallas_call(
        functools.partial(_sb_step_kernel, pp=pp, nh=nh),
        grid_spec=grid_spec,
        out_shape=jax.ShapeDtypeStruct((nb, ntok, width), F32),
        compiler_params=_params(("arbitrary", "arbitrary")),
        name="stick_breaking_step",
    )(page_table, q_exp, k_new, v_new, bias_rows, umat, *([cache_k] * pp), *([cache_v] * pp))


def _mix_ffn_kernel(*refs, tm, wa, wr, decode_tokens):
    if decode_tokens:
        (h_ref, ya_ref, yr_ref, ys_ref, p_ref, mg_ref, wout_ref, nf_ref, wup_ref, wgate_ref,
         cw_ref, cb_ref, wdown_ref, pn_ref, pgate_ref, pproj_ref, prev1_ref, prev2_ref,
         out_ref, conv_ref, ext_scr, acc_scr) = refs
    else:
        (h_ref, ya_ref, yr_ref, ys_ref, p_ref, mg_ref, wout_ref, nf_ref, wup_ref, wgate_ref,
         cw_ref, cb_ref, wdown_ref, pn_ref, pgate_ref, pproj_ref,
         out_ref, conv_ref, ext_scr, acc_scr, carry_scr) = refs

        @pl.when(pl.program_id(1) == 0)
        def _():
            carry_scr[...] = jnp.zeros(carry_scr.shape, F32)

    mg = mg_ref[...]
    a1 = wa + wr
    merged = (_dot(_rms(ya_ref[...], mg[:, 0:wa]), wout_ref[0:wa, :])
              + _dot(_rms(yr_ref[...], mg[:, wa:a1]), wout_ref[wa:a1, :])
              + _dot(_rms(ys_ref[...], mg[:, a1:]), wout_ref[a1:, :]))
    h_mid = h_ref[...] + merged
    n2 = _rms(h_mid, nf_ref[...]).astype(BF16)
    acc_scr[...] = jnp.zeros(acc_scr.shape, F32)
    if decode_tokens:
        ext_scr[0:SUBLANES, :] = jnp.zeros((SUBLANES, FF_CHUNK), F32)
        tok = lax.broadcasted_iota(jnp.int32, (tm, FF_CHUNK), 0) % decode_tokens

    def chunk(c, _):
        u = jnp.dot(n2, wup_ref[c], preferred_element_type=F32)
        g = jnp.dot(n2, wgate_ref[c], preferred_element_type=F32)
        if not decode_tokens:
            ext_scr[0:SUBLANES, :] = carry_scr[c]
        ext_scr[SUBLANES:SUBLANES + tm, :] = u
        prev1 = ext_scr[pl.ds(SUBLANES - 1, tm), :]
        prev2 = ext_scr[pl.ds(SUBLANES - 2, tm), :]
        if decode_tokens:
            prev1 = jnp.where(tok >= 1, prev1, prev1_ref[c])
            prev2 = jnp.where(tok >= 2, prev2, prev2_ref[c])
            conv_ref[c] = u
        else:
            last = u[tm - SUBLANES:tm, :]
            carry_scr[c] = last
            conv_ref[c] = last
        cw = cw_ref[c]
        conv = cb_ref[c] + cw[0:1, :] * prev2
        conv = conv + cw[1:2, :] * prev1
        conv = conv + cw[2:3, :] * u
        acc_scr[...] += _dot(jax.nn.gelu(conv) * g, wdown_ref[c])
        return 0

    lax.fori_loop(0, wup_ref.shape[0], chunk, 0)
    h2 = h_mid + acc_scr[...]
    gate = jax.nn.sigmoid(_dot(_rms(h2, pn_ref[...]), pgate_ref[...]))
    out_ref[...] = h2 + _dot(p_ref[...], pproj_ref[...]) * gate


def _mix_ffn(h2, ya, yr, ys, p2, w, prev=None, *, nb, seq, tm, decode_tokens=0):
    d = h2.shape[1]
    wa, wr, ws = w["d_skip"].shape[1], yr.shape[1], ys.shape[1]
    pd = p2.shape[1]
    nt = seq // tm
    nchunk = w["w_up"].shape[0]
    row = lambda b, t: (b * nt + t, 0)
    tspec = lambda width: pl.BlockSpec((tm, width), row)
    ya_spec = tspec(wa) if decode_tokens else pl.BlockSpec((tm, wa), lambda b, t: (t, b))
    weights = [w["merge"], w["w_out"], w["norm_ffn"], w["w_up"], w["w_gate"], w["conv_w"],
               w["conv_b"], w["w_down"], w["ple_norm"], w["ple_gate"], w["ple_proj"]]
    in_specs = [tspec(d), ya_spec, tspec(wr), tspec(ws), tspec(pd)]
    in_specs += [_const_spec(a.shape) for a in weights]
    args = [h2, ya, yr, ys, p2] + weights
    scratch = [pltpu.VMEM((tm + SUBLANES, FF_CHUNK), F32), pltpu.VMEM((tm, d), F32)]
    if decode_tokens:
        in_specs += [_const_spec(prev[0].shape), _const_spec(prev[1].shape)]
        args += list(prev)
        conv_rows = tm
    else:
        scratch.append(pltpu.VMEM((nchunk, SUBLANES, FF_CHUNK), F32))
        conv_rows = SUBLANES
    return pl.pallas_call(
        functools.partial(_mix_ffn_kernel, tm=tm, wa=wa, wr=wr, decode_tokens=decode_tokens),
        grid=(nb, nt),
        in_specs=in_specs,
        out_specs=[tspec(d),
                   pl.BlockSpec((None, nchunk, conv_rows, FF_CHUNK), lambda b, t: (b, 0, 0, 0))],
        out_shape=[jax.ShapeDtypeStruct((nb * seq, d), F32),
                   jax.ShapeDtypeStruct((nb, nchunk, conv_rows, FF_CHUNK), F32)],
        scratch_shapes=scratch,
        compiler_params=_params(("arbitrary", "arbitrary")),
        name="mix_ffn_step" if decode_tokens else "mix_ffn",
    )(*args)


def _block_diag(blocks):
    g, r, c = blocks.shape
    eye = jnp.eye(g, dtype=blocks.dtype)
    return (eye[:, None, :, None] * blocks[:, :, None, :]).reshape(g * r, g * c)


def _s5_params(lam_re, lam_im, log_dt, b_re, b_im, c_re, c_im):
    dt = jnp.exp(log_dt.astype(F32))[:, None]
    lr = lam_re.astype(F32)
    li = lam_im.astype(F32)
    mag = jnp.exp(lr * dt)
    ang = li * dt
    ab_re = mag * jnp.cos(ang)
    ab_im = mag * jnp.sin(ang)
    den = lr * lr + li * li
    f_re = ((ab_re - 1.0) * lr + ab_im * li) / den
    f_im = (ab_im * lr - (ab_re - 1.0) * li) / den
    bb_re = f_re[..., None] * b_re - f_im[..., None] * b_im
    bb_im = f_re[..., None] * b_im + f_im[..., None] * b_re
    to_in = lambda m: _block_diag(jnp.swapaxes(m, 1, 2))
    bblk = jnp.concatenate([to_in(bb_re), to_in(bb_im)], axis=1)
    to_out = lambda m: _block_diag(jnp.swapaxes(m, 1, 2))
    cblk = jnp.concatenate([to_out(c_re), -to_out(c_im)], axis=0)
    return ab_re.reshape(1, -1), ab_im.reshape(1, -1), bblk.astype(BF16), cblk.astype(BF16)


def _ret_consts(nh, chunk, group):
    log_g = jnp.log1p(-jnp.exp2(-5.0 - jnp.arange(nh, dtype=F32)))
    idx = jnp.arange(chunk, dtype=jnp.int32)
    pos = (idx % group).astype(F32)
    diff = pos[:, None] - pos[None, :]
    ok = (diff >= 0) & ((idx // group)[:, None] == (idx // group)[None, :])
    dmask = jnp.where(ok[None], jnp.exp(jnp.where(ok, diff, 0.0)[None] * log_g[:, None, None]), 0.0)
    xi = jnp.exp((pos + 1.0)[None, :] * log_g[:, None])
    zeta = jnp.exp((group - 1.0 - pos)[None, :] * log_g[:, None])
    d_chunk = jnp.exp(group * log_g)
    lanes = lambda t: jnp.repeat(t.T, HEAD_DIM, axis=1)
    return dmask, lanes(xi), lanes(zeta), jnp.repeat(d_chunk, HEAD_DIM)[None, :]


def _rope_tables(pos, nh):
    half = HEAD_DIM // 2
    inv = jnp.power(ROPE_BASE, -jnp.arange(half, dtype=F32) / half)
    ang = pos.astype(F32)[:, None] * inv[None, :]
    cos = jnp.cos(ang)
    sin = jnp.sin(ang)
    return (jnp.tile(jnp.concatenate([cos, cos], axis=1), (1, nh)),
            jnp.tile(jnp.concatenate([-sin, sin], axis=1), (1, nh)))


def _group_mean_matrix(width):
    g = jnp.arange(width) // HEAD_DIM
    return jnp.where(g[:, None] == g[None, :], 1.0 / HEAD_DIM, 0.0).astype(BF16)


def _later_matrix(n):
    i = jnp.arange(n)
    return (i[:, None] > i[None, :]).astype(BF16)


def _chunk_cols(m):
    r, f = m.shape
    return m.reshape(r, f // FF_CHUNK, FF_CHUNK).transpose(1, 0, 2)


def _unchunk_cols(m):
    b, n, r, c = m.shape
    return m.transpose(0, 2, 1, 3).reshape(b, r, n * c)


def _layer_weights(i, wts):
    (norm_mix_gain, w_in, lam_re, lam_im, log_dt, b_re, b_im, c_re, c_im, s5_d, glu_w, glu_b,
     ret_gn, q_gain, k_gain, sb_bias, merge, w_out, norm_ffn, w_up, w_gate, conv_w, conv_b, w_down,
     ple_norm, ple_gate, ple_proj) = [t[i] for t in wts]
    nh_s = sb_bias.shape[0]
    row = lambda v: v.reshape(1, -1)
    are, aim, bblk, cblk = _s5_params(lam_re, lam_im, log_dt, b_re, b_im, c_re, c_im)
    conv_w8 = jnp.concatenate([conv_w, jnp.zeros((SUBLANES - CONV_W, conv_w.shape[1]), F32)], axis=0)
    return dict(
        norm_mix=row(norm_mix_gain), w_in=w_in.astype(BF16),
        are=are, aim=aim, bblk=bblk, cblk=cblk, d_skip=row(s5_d), glu_w=glu_w.astype(BF16),
        glu_b=row(glu_b), ret_gn=row(ret_gn),
        q_gain=row(jnp.tile(q_gain, nh_s)), k_gain=row(jnp.tile(k_gain, nh_s)), sb_bias=sb_bias,
        merge=row(merge), w_out=w_out.astype(BF16), norm_ffn=row(norm_ffn),
        w_up=_chunk_cols(w_up.astype(BF16)), w_gate=_chunk_cols(w_gate.astype(BF16)),
        conv_w=_chunk_cols(conv_w8), conv_b=_chunk_cols(row(conv_b)),
        w_down=w_down.astype(BF16).reshape(-1, FF_CHUNK, w_down.shape[1]),
        ple_norm=row(ple_norm), ple_gate=ple_gate.astype(BF16), ple_proj=ple_proj.astype(BF16))


def _prompt_layer(h2, p2, w, consts, *, nb, seq):
    wa, wr, ws, nh_r = consts["wa"], consts["wr"], consts["ws"], consts["nh_r"]
    tm, tq, chunk, tc = consts["tm"], consts["tq"], consts["chunk"], consts["tc"]
    ua, qr, kr, vr, gr, qs, ks, vs = _proj_in(
        h2, w["norm_mix"], w["w_in"], consts["cos"], consts["sin"], w["q_gain"], w["k_gain"],
        consts["gm_s"], nb=nb, seq=seq, tm=tm, wa=wa, wr=wr, ws=ws)
    zeros = jnp.zeros((nb, w["are"].shape[1]), F32)
    ya, sre, sim = _s5(ua.reshape(seq * nb, wa), zeros, zeros, w["are"], w["aim"], w["bblk"],
                       w["cblk"], w["d_skip"], w["glu_w"], w["glu_b"], nb=nb, seq=seq, tc=tc)
    yr, sret = _retention(qr, kr, vr, gr, consts["dmask"], consts["xi"], consts["zeta"],
                          consts["dcol"], w["ret_gn"], consts["gm_r"], nb=nb, seq=seq,
                          chunk=chunk, nh=nh_r)
    ys = _stick_breaking(qs, ks, vs, w["sb_bias"], consts["later_q"], nb=nb, seq=seq, tq=tq)
    h_out, conv = _mix_ffn(h2, ya.reshape(seq, nb * wa), yr, ys, p2, w, nb=nb, seq=seq, tm=tm)
    ret_state = jnp.stack([sret[:, h * HEAD_DIM:(h + 1) * HEAD_DIM, h * HEAD_DIM:(h + 1) * HEAD_DIM]
                           for h in range(nh_r)], axis=1)
    conv_state = _unchunk_cols(conv)[:, SUBLANES - (CONV_W - 1):, :]
    return h_out, (ks, vs, sre, sim, ret_state, conv_state)


def _decode_layer(h2, p2, w, consts, past, *, layer, nb, ntok):
    wa, wr, ws, nh_r, nh_s = consts["wa"], consts["wr"], consts["ws"], consts["nh_r"], consts["nh_s"]
    rows = nb * ntok
    ua, qr, kr, vr, gr, qs, ks, vs = _proj_in(
        h2, w["norm_mix"], w["w_in"], consts["cos"], consts["sin"], w["q_gain"], w["k_gain"],
        consts["gm_s"], nb=1, seq=rows, tm=rows, wa=wa, wr=wr, ws=ws)
    to_tm = lambda a: a.reshape(nb, ntok, -1).transpose(1, 0, 2).reshape(rows, -1)
    from_tm = lambda a: a.reshape(ntok, nb, -1).transpose(1, 0, 2).reshape(rows, -1)
    ya, sre, sim = _s5(to_tm(ua), past["ssm_re"], past["ssm_im"], w["are"], w["aim"], w["bblk"],
                       w["cblk"], w["d_skip"], w["glu_w"], w["glu_b"], nb=nb, seq=ntok, tc=ntok)
    s0cat = past["ret"].transpose(0, 2, 1, 3).reshape(nb * HEAD_DIM, wr)
    yr, snew = _retention_step(qr, kr, vr, gr, consts["dmask"], consts["xi"], consts["zeta"],
                               consts["dcol"], w["ret_gn"], consts["gm_r"], s0cat,
                               consts["expand"], nh=nh_r, ntok=ntok)
    ret_state = snew.reshape(nb, HEAD_DIM, nh_r, HEAD_DIM).transpose(0, 2, 1, 3)
    q_rep = jnp.repeat(qs.reshape(nb, ntok, ws), nh_s, axis=1)
    q_exp = jnp.where(consts["q_own"], q_rep, 0.0)
    pad = lambda a: jnp.pad(a.reshape(nb, ntok, ws).transpose(0, 2, 1),
                            ((0, 0), (0, 0), (0, PAGE_SIZE - ntok)))
    bias_rows = jnp.tile(w["sb_bias"], ntok).reshape(ntok * nh_s, 1)
    ys = _stick_breaking_step(past["page_table"], q_exp, pad(ks), pad(vs), bias_rows,
                              consts["later_p"], past["cache_k"], past["cache_v"], layer=layer,
                              pp=consts["pp"], nh=nh_s).reshape(rows, ws)
    buf = past["conv"]
    zero = jnp.zeros_like(buf[:, :1])
    prev1 = jnp.concatenate([buf[:, 1:2]] + [zero] * (ntok - 1), axis=1).reshape(rows, -1)
    prev2 = jnp.concatenate([buf[:, 0:1], buf[:, 1:2]] + [zero] * (ntok - 2), axis=1).reshape(rows, -1)
    h_out, conv = _mix_ffn(h2, from_tm(ya), yr, ys, p2, w, (_chunk_cols(prev1), _chunk_cols(prev2)),
                           nb=1, seq=rows, tm=rows, decode_tokens=ntok)
    conv_state = _unchunk_cols(conv).reshape(nb, ntok, -1)[:, ntok - (CONV_W - 1):, :]
    return h_out, (ks, vs, sre, sim, ret_state, conv_state)


def kernel(x_prompt, x_sample, cache_k, cache_v, state_ssm_re, state_ssm_im, state_ret, state_conv, page_table, p_prompt, p_sample, norm_mix_gain, w_in, s5_lambda_re, s5_lambda_im, s5_log_dt, s5_b_re, s5_b_im, s5_c_re, s5_c_im, s5_d, s5_glu_w, s5_glu_b, ret_gn_gain, sb_q_gain, sb_k_gain, sb_logit_bias, merge_gain, w_out, norm_ffn_gain, ffn_w_up, ffn_w_gate, ffn_conv_w, ffn_conv_b, ffn_w_down, ple_norm_gain, ple_w_gate, ple_w_proj):
    wts = (norm_mix_gain, w_in, s5_lambda_re, s5_lambda_im, s5_log_dt, s5_b_re, s5_b_im, s5_c_re,
           s5_c_im, s5_d, s5_glu_w, s5_glu_b, ret_gn_gain, sb_q_gain, sb_k_gain, sb_logit_bias,
           merge_gain, w_out, norm_ffn_gain, ffn_w_up, ffn_w_gate, ffn_conv_w, ffn_conv_b,
           ffn_w_down, ple_norm_gain, ple_w_gate, ple_w_proj)
    depth = w_in.shape[0]
    nb, seq, d = x_prompt.shape
    nbs, ntok, _ = x_sample.shape
    past_len = page_table.shape[1] * PAGE_SIZE
    wa = s5_d.shape[1]
    wr = ret_gn_gain.shape[1]
    nh_r = wr // HEAD_DIM
    nh_s = sb_logit_bias.shape[1]
    ws = nh_s * HEAD_DIM
    n_pool = cache_k.shape[1]
    rows_s = nbs * ntok

    shared = dict(wa=wa, wr=wr, ws=ws, nh_r=nh_r, nh_s=nh_s, gm_s=_group_mean_matrix(ws),
                  gm_r=_group_mean_matrix(wr))
    chunk = min(128, seq)
    cos_p, sin_p = _rope_tables(jnp.arange(seq, dtype=jnp.int32), nh_r)
    dmask_p, xi_p, zeta_p, dcol_p = _ret_consts(nh_r, chunk, chunk)
    tq = min(256, seq)
    consts_p = dict(shared, cos=cos_p, sin=sin_p, dmask=dmask_p, xi=xi_p, zeta=zeta_p, dcol=dcol_p,
                    tm=min(512, seq), tq=tq, chunk=chunk, tc=min(128, seq),
                    later_q=_later_matrix(tq))
    pos_s = past_len + jnp.tile(jnp.arange(ntok, dtype=jnp.int32), nbs)
    cos_s, sin_s = _rope_tables(pos_s, nh_r)
    dmask_s, xi_s, zeta_s, dcol_s = _ret_consts(nh_r, rows_s, ntok)
    lane_src = jnp.arange(wr)
    lane_dst = jnp.arange(nbs * HEAD_DIM)
    expand = jnp.stack([(lane_src[:, None] == h * HEAD_DIM + lane_dst[None, :] % HEAD_DIM)
                        for h in range(nh_r)]).astype(BF16)
    q_own = (jnp.arange(ws)[None, :] // HEAD_DIM) == (jnp.arange(ntok * nh_s)[:, None] % nh_s)
    consts_s = dict(shared, cos=cos_s, sin=sin_s, dmask=dmask_s, xi=xi_s, zeta=zeta_s, dcol=dcol_s,
                    expand=expand, q_own=q_own[None], later_p=_later_matrix(PAGE_SIZE),
                    pp=min(8, page_table.shape[1]))

    ck = cache_k.transpose(0, 1, 3, 4, 2).reshape(depth, n_pool, ws, PAGE_SIZE)
    cv = cache_v.transpose(0, 1, 3, 4, 2).reshape(depth, n_pool, ws, PAGE_SIZE)
    h_p = x_prompt.reshape(nb * seq, d)
    h_s = x_sample.reshape(rows_s, d)
    outs_p, outs_s = [], []
    for i in range(depth):
        w = _layer_weights(i, wts)
        h_p, st_p = _prompt_layer(h_p, p_prompt[i].reshape(nb * seq, -1), w, consts_p, nb=nb, seq=seq)
        past = dict(page_table=page_table, cache_k=ck, cache_v=cv,
                    ssm_re=state_ssm_re[i].reshape(nbs, -1), ssm_im=state_ssm_im[i].reshape(nbs, -1),
                    ret=state_ret[i], conv=state_conv[i])
        h_s, st_s = _decode_layer(h_s, p_sample[i].reshape(rows_s, -1), w, consts_s, past,
                                  layer=i, nb=nbs, ntok=ntok)
        outs_p.append(st_p)
        outs_s.append(st_s)

    def stack(outs, b, t):
        k, v, sre, sim, ret, conv = [jnp.stack(x, axis=0) for x in zip(*outs)]
        kv_shape = (depth, b, t, nh_s, HEAD_DIM)
        st_shape = (depth, b, wa // S5_CH, S5_STATE)
        return (k.reshape(kv_shape), v.reshape(kv_shape), sre.reshape(st_shape),
                sim.reshape(st_shape), ret, conv)

    return ((h_p.reshape(nb, seq, d), h_s.reshape(nbs, ntok, d))
            + stack(outs_p, nb, seq) + stack(outs_s, nbs, ntok))
```

```python
import functools

import jax
import jax.numpy as jnp
from jax import lax
from jax.experimental import pallas as pl
from jax.experimental.pallas import tpu as pltpu

F32 = jnp.float32
BF16 = jnp.bfloat16

EPS = 1e-6
LOG2E = 1.4426950408889634
ROPE_BASE = 10000.0
HEAD_DIM = 64
S5_CH = 16
S5_STATE = 64
PAGE_SIZE = 128
CONV_W = 3
LANES = 128
SUBLANES = 8
FF_CHUNK = 256
PAGES_PER_STEP = 16
VMEM_LIMIT = 56 * 1024 * 1024

NT_DIMS = (((1,), (1,)), ((), ()))


def _dot(a, b):
    return jnp.dot(a.astype(BF16), b.astype(BF16), preferred_element_type=F32)


def _dot_nt(a, b):
    return lax.dot_general(a.astype(BF16), b.astype(BF16), NT_DIMS, preferred_element_type=F32)


def _dot_split(x, m):
    hi = x.astype(BF16)
    lo = (x - hi.astype(F32)).astype(BF16)
    return (jnp.dot(hi, m, preferred_element_type=F32)
            + jnp.dot(lo, m, preferred_element_type=F32))


def _rms(x, gain):
    return x * lax.rsqrt(jnp.mean(x * x, axis=-1, keepdims=True) + EPS) * gain


def _params(sem):
    return pltpu.CompilerParams(dimension_semantics=sem, vmem_limit_bytes=VMEM_LIMIT)


def _const_spec(shape, single_buffer=True):
    n = len(shape)
    if single_buffer:
        return pl.BlockSpec(shape, lambda *_: (0,) * n, pipeline_mode=pl.Buffered(1))
    return pl.BlockSpec(shape, lambda *_: (0,) * n)


def _proj_in_kernel(h_ref, g_ref, w_ref, cos_ref, sin_ref, qg_ref, kg_ref, gm_ref,
                    ua_ref, qr_ref, kr_ref, vr_ref, gr_ref, qs_ref, ks_ref, vs_ref,
                    *, wa, wr, ws):
    n = _rms(h_ref[...], g_ref[...]).astype(BF16)

    def seg(lo, width):
        return jnp.dot(n, w_ref[:, lo:lo + width], preferred_element_type=F32)

    cos = cos_ref[...]
    sin = sin_ref[...]
    lane = lax.broadcasted_iota(jnp.int32, (1, wr), 1)
    first_half = (lane & (HEAD_DIM - 1)) < HEAD_DIM // 2

    def rope(v):
        swapped = jnp.where(first_half, pltpu.roll(v, wr - HEAD_DIM // 2, 1),
                            pltpu.roll(v, HEAD_DIM // 2, 1))
        return v * cos + swapped * sin

    gm = gm_ref[...]

    def head_rms(v, gain):
        ms = jnp.dot((v * v).astype(BF16), gm, preferred_element_type=F32)
        return v * lax.rsqrt(ms + EPS) * gain

    o = 0
    ua_ref[...] = seg(o, wa)
    o += wa
    qr_ref[...] = rope(seg(o, wr))
    o += wr
    kr_ref[...] = rope(seg(o, wr)) * (HEAD_DIM ** -0.5)
    o += wr
    vr_ref[...] = seg(o, wr)
    o += wr
    gr_ref[...] = seg(o, wr)
    o += wr
    qs_ref[...] = head_rms(seg(o, ws), qg_ref[...]) * (LOG2E * HEAD_DIM ** -0.5)
    o += ws
    ks_ref[...] = head_rms(seg(o, ws), kg_ref[...])
    o += ws
    vs_ref[...] = seg(o, ws)


def _proj_in(h2, gain, w_in, cos_t, sin_t, q_gain, k_gain, gm, *, nb, seq, tm, wa, wr, ws):
    d = h2.shape[1]
    nt = seq // tm
    row = lambda b, t: (b * nt + t, 0)
    tspec = lambda w: pl.BlockSpec((tm, w), row)
    out_shape = [jax.ShapeDtypeStruct((seq, nb * wa), F32)]
    out_shape += [jax.ShapeDtypeStruct((nb * seq, wr), F32)] * 4
    out_shape += [jax.ShapeDtypeStruct((nb * seq, ws), F32)] * 3
    return pl.pallas_call(
        functools.partial(_proj_in_kernel, wa=wa, wr=wr, ws=ws),
        grid=(nb, nt),
        in_specs=[tspec(d), _const_spec((1, d)), _const_spec(w_in.shape),
                  pl.BlockSpec((tm, wr), lambda b, t: (t, 0)),
                  pl.BlockSpec((tm, wr), lambda b, t: (t, 0)),
                  _const_spec((1, ws)), _const_spec((1, ws)), _const_spec((ws, ws))],
        out_specs=[pl.BlockSpec((tm, wa), lambda b, t: (t, b))]
        + [tspec(wr)] * 4 + [tspec(ws)] * 3,
        out_shape=out_shape,
        compiler_params=_params(("arbitrary", "arbitrary")),
        name="proj_in",
    )(h2, gain, w_in, cos_t, sin_t, q_gain, k_gain, gm)


def _s5_kernel(u_ref, s0re_ref, s0im_ref, are_ref, aim_ref, bblk_ref, cblk_ref, dsk_ref,
               gw_ref, gb_ref, y_ref, fre_ref, fim_ref, drv_scr, hre_scr, him_scr, *, nb, tc):
    ns = hre_scr.shape[1]

    @pl.when(pl.program_id(0) == 0)
    def _():
        hre_scr[...] = s0re_ref[...]
        him_scr[...] = s0im_ref[...]

    u = u_ref[...]
    drv_scr[...] = _dot(u, bblk_ref[...])
    are = jnp.broadcast_to(are_ref[...], (nb, ns))
    aim = jnp.broadcast_to(aim_ref[...], (nb, ns))

    def step(t, carry):
        hre, him = carry
        rows = pl.ds(pl.multiple_of(t * nb, nb), nb)
        nre = are * hre - aim * him + drv_scr[rows, 0:ns]
        nim = are * him + aim * hre + drv_scr[rows, ns:2 * ns]
        drv_scr[rows, 0:ns] = nre
        drv_scr[rows, ns:2 * ns] = nim
        return nre, nim

    hre, him = lax.fori_loop(0, tc, step, (hre_scr[...], him_scr[...]))
    hre_scr[...] = hre
    him_scr[...] = him
    fre_ref[...] = hre
    fim_ref[...] = him
    y = _dot(drv_scr[...], cblk_ref[...]) + dsk_ref[...] * u
    gy = jax.nn.gelu(y)
    y_ref[...] = gy * jax.nn.sigmoid(_dot(gy, gw_ref[...]) + gb_ref[...])


def _s5(u_tm, s0re, s0im, are, aim, bblk, cblk, dsk, gw, gb, *, nb, seq, tc):
    wa = u_tm.shape[1]
    ns = are.shape[1]
    rows = tc * nb
    return pl.pallas_call(
        functools.partial(_s5_kernel, nb=nb, tc=tc),
        grid=(seq // tc,),
        in_specs=[pl.BlockSpec((rows, wa), lambda i: (i, 0)),
                  _const_spec((nb, ns)), _const_spec((nb, ns)),
                  _const_spec((1, ns)), _const_spec((1, ns)),
                  _const_spec(bblk.shape), _const_spec(cblk.shape), _const_spec((1, wa)),
                  _const_spec(gw.shape), _const_spec((1, wa))],
        out_specs=[pl.BlockSpec((rows, wa), lambda i: (i, 0)),
                   pl.BlockSpec((nb, ns), lambda i: (0, 0)),
                   pl.BlockSpec((nb, ns), lambda i: (0, 0))],
        out_shape=[jax.ShapeDtypeStruct((seq * nb, wa), F32),
                   jax.ShapeDtypeStruct((nb, ns), F32),
                   jax.ShapeDtypeStruct((nb, ns), F32)],
        scratch_shapes=[pltpu.VMEM((rows, 2 * ns), F32), pltpu.VMEM((nb, ns), F32),
                        pltpu.VMEM((nb, ns), F32)],
        compiler_params=_params(("arbitrary",)),
        name="s5",
    )(u_tm, s0re, s0im, are, aim, bblk, cblk, dsk, gw, gb)


def _head_lane_mask(width, h):
    lane = lax.broadcasted_iota(jnp.int32, (1, width), 1)
    return (lane >> 6) == h


def _ret_intra(q, k, v, dmask_ref, nh):
    wr = q.shape[1]
    o = jnp.zeros(q.shape, F32)
    for h in range(nh):
        mh = _head_lane_mask(wr, h)
        s = _dot_nt(jnp.where(mh, q, 0.0), k) * dmask_ref[h]
        o = jnp.where(mh, _dot(s, v), o)
    return o


def _ret_finish(o, gate, gn_ref, gm_ref):
    gm = gm_ref[...]
    xc = o - _dot_split(o, gm)
    var = _dot_split(xc * xc, gm)
    return jax.nn.silu(gate) * (xc * lax.rsqrt(var + EPS) * gn_ref[...])


def _ret_kernel(q_ref, k_ref, v_ref, g_ref, dmask_ref, xi_ref, zeta_ref, dcol_ref, gn_ref, gm_ref,
                y_ref, sfin_ref, s_scr, *, nh):
    wr = q_ref.shape[1]

    @pl.when(pl.program_id(1) == 0)
    def _():
        s_scr[...] = jnp.zeros(s_scr.shape, F32)

    q = q_ref[...]
    k = k_ref[...]
    v = v_ref[...]
    s_prev = s_scr[...]
    o = _ret_intra(q, k, v, dmask_ref, nh) + _dot(q * xi_ref[...], s_prev)
    row_h = lax.broadcasted_iota(jnp.int32, (wr, wr), 0) >> 6
    col_h = lax.broadcasted_iota(jnp.int32, (wr, wr), 1) >> 6
    kv = _dot((k * zeta_ref[...]).T, v)
    s_new = s_prev * dcol_ref[...] + jnp.where(row_h == col_h, kv, 0.0)
    s_scr[...] = s_new
    sfin_ref[...] = s_new
    y_ref[...] = _ret_finish(o, g_ref[...], gn_ref, gm_ref)


def _retention(q, k, v, g, dmask, xi, zeta, dcol, gn, gm, *, nb, seq, chunk, nh):
    wr = q.shape[1]
    nc = seq // chunk
    tspec = pl.BlockSpec((chunk, wr), lambda b, c: (b * nc + c, 0))
    return pl.pallas_call(
        functools.partial(_ret_kernel, nh=nh),
        grid=(nb, nc),
        in_specs=[tspec] * 4 + [_const_spec(dmask.shape), _const_spec(xi.shape),
                                _const_spec(zeta.shape), _const_spec((1, wr)),
                                _const_spec((1, wr)), _const_spec((wr, wr))],
        out_specs=[tspec, pl.BlockSpec((None, wr, wr), lambda b, c: (b, 0, 0))],
        out_shape=[jax.ShapeDtypeStruct((nb * seq, wr), F32),
                   jax.ShapeDtypeStruct((nb, wr, wr), F32)],
        scratch_shapes=[pltpu.VMEM((wr, wr), F32)],
        compiler_params=_params(("arbitrary", "arbitrary")),
        name="retention",
    )(q, k, v, g, dmask, xi, zeta, dcol, gn, gm)


def _ret_step_kernel(q_ref, k_ref, v_ref, g_ref, dmask_ref, xi_ref, zeta_ref, dcol_ref, gn_ref,
                     gm_ref, s0_ref, exp_ref, y_ref, snew_ref, *, nh, ntok):
    t_rows, wr = q_ref.shape
    sw = s0_ref.shape[0]
    q = q_ref[...]
    k = k_ref[...]
    v = v_ref[...]
    s0 = s0_ref[...]
    o = _ret_intra(q, k, v, dmask_ref, nh)
    qx = q * xi_ref[...]
    kz = k * zeta_ref[...]
    own_b = ((lax.broadcasted_iota(jnp.int32, (t_rows, sw), 1) >> 6)
             == lax.broadcasted_iota(jnp.int32, (t_rows, sw), 0) // ntok)
    s_new = s0 * dcol_ref[...]
    for h in range(nh):
        mh = _head_lane_mask(wr, h)
        q_exp = jnp.where(own_b, _dot(jnp.where(mh, qx, 0.0), exp_ref[h]), 0.0)
        k_exp = jnp.where(own_b, _dot(jnp.where(mh, kz, 0.0), exp_ref[h]), 0.0)
        o = o + jnp.where(mh, _dot(q_exp, s0), 0.0)
        s_new = s_new + jnp.where(mh, _dot(k_exp.T, v), 0.0)
    snew_ref[...] = s_new
    y_ref[...] = _ret_finish(o, g_ref[...], gn_ref, gm_ref)


def _retention_step(q, k, v, g, dmask, xi, zeta, dcol, gn, gm, s0cat, expand, *, nh, ntok):
    t_rows, wr = q.shape
    args = (q, k, v, g, dmask, xi, zeta, dcol, gn, gm, s0cat, expand)
    return pl.pallas_call(
        functools.partial(_ret_step_kernel, nh=nh, ntok=ntok),
        grid=(1,),
        in_specs=[_const_spec(a.shape) for a in args],
        out_specs=[_const_spec((t_rows, wr), False), _const_spec(s0cat.shape, False)],
        out_shape=[jax.ShapeDtypeStruct((t_rows, wr), F32),
                   jax.ShapeDtypeStruct(s0cat.shape, F32)],
        compiler_params=_params(("arbitrary",)),
        name="retention_step",
    )(*args)


def _sb_fail(z2, causal):
    sign = jnp.uint32(0x80000000)
    neg_abs = lax.bitcast_convert_type(lax.bitcast_convert_type(z2, jnp.uint32) | sign, F32)
    g = jnp.maximum(z2, 0.0) + jnp.log(1.0 + jnp.exp2(neg_abs)) * LOG2E
    return g if causal is None else jnp.where(causal, g, 0.0)


def _sb_weight(z2, suffix, run, causal):
    w = jnp.exp2(z2 - suffix - run)
    return w if causal is None else jnp.where(causal, w, 0.0)


def _sb_kernel(bias_ref, q_ref, k_ref, v_ref, u_ref, o_ref, *, tq):
    hp = pl.program_id(1)
    qi = pl.program_id(2)
    q = q_ref[...]
    width = q.shape[1]
    nh = width // HEAD_DIM
    rows = nh * tq
    umat = u_ref[...]
    q2 = jnp.concatenate([jnp.where(_head_lane_mask(width, h), q, 0.0) for h in range(nh)],
                         axis=0).astype(BF16)
    row_h = lax.broadcasted_iota(jnp.int32, (rows, 1), 0) // tq
    bias = jnp.zeros((rows, 1), F32)
    for h in range(nh):
        bias = jnp.where(row_h == h, bias_ref[hp * nh + h] * LOG2E, bias)
    causal = (lax.broadcasted_iota(jnp.int32, (rows, tq), 1)
              < (lax.broadcasted_iota(jnp.int32, (rows, tq), 0) & (tq - 1)))

    def tiles(j, count, carry, causal_mask):
        acc, run = carry
        keys = [pl.ds(pl.multiple_of((j - i) * tq, tq), tq) for i in range(count)]
        zs = [_dot_nt(q2, k_ref[ks, :]) + bias for ks in keys]
        suffixes = [_dot(_sb_fail(z, causal_mask), umat) for z in zs]
        for z, suffix, ks in zip(zs, suffixes, keys):
            acc = acc + _dot(_sb_weight(z, suffix, run, causal_mask), v_ref[ks, :])
            run = run + suffix[:, 0:1]
        return acc, run

    carry = (jnp.zeros((rows, width), F32), jnp.zeros((rows, 1), F32))
    carry = tiles(qi, 1, carry, causal)
    one = qi & 1
    two = (qi >> 1) & 1
    carry = lax.fori_loop(0, one, lambda n, c: tiles(qi - 1, 1, c, None), carry)
    carry = lax.fori_loop(0, two, lambda n, c: tiles(qi - 1 - one, 2, c, None), carry)
    top = qi - 1 - one - 2 * two
    acc, _ = lax.fori_loop(0, qi >> 2, lambda n, c: tiles(top - 4 * n, 4, c, None), carry)
    out = acc[0:tq]
    for h in range(1, nh):
        out = jnp.where(_head_lane_mask(width, h), acc[h * tq:(h + 1) * tq], out)
    o_ref[...] = out


def _stick_breaking(q, k, v, bias, umat, *, nb, seq, tq):
    ws = q.shape[1]
    nq = seq // tq
    qspec = pl.BlockSpec((tq, LANES), lambda b, hp, i: (b * nq + i, hp))
    kspec = pl.BlockSpec((seq, LANES), lambda b, hp, i: (b, hp))
    return pl.pallas_call(
        functools.partial(_sb_kernel, tq=tq),
        grid=(nb, ws // LANES, nq),
        in_specs=[pl.BlockSpec(memory_space=pltpu.SMEM), qspec, kspec, kspec,
                  _const_spec((tq, tq))],
        out_specs=qspec,
        out_shape=jax.ShapeDtypeStruct((nb * seq, ws), F32),
        compiler_params=_params(("arbitrary", "arbitrary", "arbitrary")),
        name="stick_breaking",
    )(bias, q, k, v, umat)


def _sb_step_kernel(pt_ref, q_ref, kn_ref, vn_ref, bias_ref, u_ref, *rest, pp, nh):
    k_pages = rest[:pp]
    v_pages = rest[pp:2 * pp]
    o_ref = rest[2 * pp]
    acc_scr, run_scr = rest[2 * pp + 1:]
    j = pl.program_id(1)
    q = q_ref[...].astype(BF16)
    rows, width = q.shape
    bias = bias_ref[...] * LOG2E
    umat = u_ref[...]

    def sweep(z_pages, causal):
        n = len(z_pages)
        z = jnp.concatenate(z_pages, axis=0) + jnp.concatenate([bias] * n, axis=0)
        suffix = _dot(_sb_fail(z, causal), umat)
        run = run_scr[...]
        later = [None] * n
        for p in reversed(range(n)):
            later[p] = run
            run = run + suffix[p * rows:(p + 1) * rows, 0:1]
        w = _sb_weight(z, suffix, jnp.concatenate(later, axis=0), causal)
        run_scr[...] = run
        return [w[p * rows:(p + 1) * rows] for p in range(n)]

    @pl.when(j == 0)
    def _():
        run_scr[...] = jnp.zeros(run_scr.shape, F32)
        pad = jnp.zeros((PAGE_SIZE - kn_ref.shape[0], width), F32)
        kn = jnp.concatenate([kn_ref[...], pad], axis=0)
        vn = jnp.concatenate([vn_ref[...], pad], axis=0)
        row_t = lax.broadcasted_iota(jnp.int32, (rows, PAGE_SIZE), 0) // nh
        col = lax.broadcasted_iota(jnp.int32, (rows, PAGE_SIZE), 1)
        (w,) = sweep([_dot_nt(q, kn)], col < row_t)
        acc_scr[...] = _dot(w, vn)

    @pl.when(j > 0)
    def _():
        kt = jnp.concatenate([k[...].astype(BF16) for k in k_pages], axis=1)
        z = jnp.dot(q, kt, preferred_element_type=F32)
        w = sweep([z[:, p * PAGE_SIZE:(p + 1) * PAGE_SIZE] for p in range(pp)], None)
        vt = jnp.concatenate([v[...].astype(BF16) for v in v_pages], axis=1)
        acc_scr[...] += _dot_nt(jnp.concatenate(w, axis=1), vt)

    @pl.when(j == pl.num_programs(1) - 1)
    def _():
        lane_h = lax.broadcasted_iota(jnp.int32, (rows, width), 1) >> 6
        row_h = lax.broadcasted_iota(jnp.int32, (rows, width), 0) % nh
        own = jnp.where(lane_h == row_h, acc_scr[...], 0.0)
        o_ref[...] = jnp.sum(own.reshape(rows // nh, nh, width), axis=1)


def _stick_breaking_step(page_table, q_exp, k_new, v_new, bias_rows, umat, cache_k, cache_v, *,
                         layer, pp, nh):
    nb, rows, width = q_exp.shape
    n_pages = page_table.shape[1]
    n_steps = n_pages // pp
    ntok = rows // nh

    def page_map(slot):
        def index(b, j, pt):
            return (layer, pt[b, n_pages - jnp.maximum(j, 1) * pp + slot], 0, 0)
        return index

    page_specs = [pl.BlockSpec((None, None, width, PAGE_SIZE), page_map(s)) for s in range(pp)]
    per_b = lambda r: pl.BlockSpec((None, r, width), lambda b, j, pt: (b, 0, 0))
    new_spec = per_b(k_new.shape[1])
    grid_spec = pltpu.PrefetchScalarGridSpec(
        num_scalar_prefetch=1,
        grid=(nb, n_steps + 1),
        in_specs=[per_b(rows), new_spec, new_spec,
                  pl.BlockSpec((rows, 1), lambda b, j, pt: (0, 0)),
                  pl.BlockSpec((PAGE_SIZE, PAGE_SIZE), lambda b, j, pt: (0, 0))]
        + page_specs + page_specs,
        out_specs=per_b(ntok),
        scratch_shapes=[pltpu.VMEM((rows, width), F32), pltpu.VMEM((rows, 1), F32)],
    )
    return pl.pallas_call(
        functools.partial(_sb_step_kernel, pp=pp, nh=nh),
        grid_spec=grid_spec,
        out_shape=jax.ShapeDtypeStruct((nb, ntok, width), F32),
        compiler_params=_params(("arbitrary", "arbitrary")),
        name="stick_breaking_step",
    )(page_table, q_exp, k_new, v_new, bias_rows, umat, *([cache_k] * pp), *([cache_v] * pp))


def _mix_ffn_kernel(*refs, tm, wa, wr, decode_tokens):
    if decode_tokens:
        (h_ref, ya_ref, yr_ref, ys_ref, p_ref, mg_ref, wout_ref, nf_ref, wup_ref, wgate_ref,
         cw_ref, cb_ref, wdown_ref, pn_ref, pgate_ref, pproj_ref, prev1_ref, prev2_ref,
         out_ref, conv_ref, ext_scr, acc_scr) = refs
    else:
        (h_ref, ya_ref, yr_ref, ys_ref, p_ref, mg_ref, wout_ref, nf_ref, wup_ref, wgate_ref,
         cw_ref, cb_ref, wdown_ref, pn_ref, pgate_ref, pproj_ref,
         out_ref, conv_ref, ext_scr, acc_scr, carry_scr) = refs

        @pl.when(pl.program_id(1) == 0)
        def _():
            carry_scr[...] = jnp.zeros(carry_scr.shape, F32)

    mg = mg_ref[...]
    a1 = wa + wr
    merged = (_dot(_rms(ya_ref[...], mg[:, 0:wa]), wout_ref[0:wa, :])
              + _dot(_rms(yr_ref[...], mg[:, wa:a1]), wout_ref[wa:a1, :])
              + _dot(_rms(ys_ref[...], mg[:, a1:]), wout_ref[a1:, :]))
    h_mid = h_ref[...] + merged
    n2 = _rms(h_mid, nf_ref[...]).astype(BF16)
    acc_scr[...] = jnp.zeros(acc_scr.shape, F32)
    if decode_tokens:
        ext_scr[0:SUBLANES, :] = jnp.zeros((SUBLANES, FF_CHUNK), F32)
        tok = lax.broadcasted_iota(jnp.int32, (tm, FF_CHUNK), 0) % decode_tokens

    def chunk(c, _):
        u = jnp.dot(n2, wup_ref[c], preferred_element_type=F32)
        g = jnp.dot(n2, wgate_ref[c], preferred_element_type=F32)
        if not decode_tokens:
            ext_scr[0:SUBLANES, :] = carry_scr[c]
        ext_scr[SUBLANES:SUBLANES + tm, :] = u
        prev1 = ext_scr[pl.ds(SUBLANES - 1, tm), :]
        prev2 = ext_scr[pl.ds(SUBLANES - 2, tm), :]
        if decode_tokens:
            prev1 = jnp.where(tok >= 1, prev1, prev1_ref[c])
            prev2 = jnp.where(tok >= 2, prev2, prev2_ref[c])
            conv_ref[c] = u
        else:
            last = u[tm - SUBLANES:tm, :]
            carry_scr[c] = last
            conv_ref[c] = last
        cw = cw_ref[c]
        conv = cb_ref[c] + cw[0:1, :] * prev2
        conv = conv + cw[1:2, :] * prev1
        conv = conv + cw[2:3, :] * u
        acc_scr[...] += _dot(jax.nn.gelu(conv) * g, wdown_ref[c])
        return 0

    lax.fori_loop(0, wup_ref.shape[0], chunk, 0, unroll=True)
    h2 = h_mid + acc_scr[...]
    gate = jax.nn.sigmoid(_dot(_rms(h2, pn_ref[...]), pgate_ref[...]))
    out_ref[...] = h2 + _dot(p_ref[...], pproj_ref[...]) * gate


def _mix_ffn(h2, ya, yr, ys, p2, w, prev=None, *, nb, seq, tm, decode_tokens=0):
    d = h2.shape[1]
    wa, wr, ws = w["d_skip"].shape[1], yr.shape[1], ys.shape[1]
    pd = p2.shape[1]
    nt = seq // tm
    nchunk = w["w_up"].shape[0]
    row = lambda b, t: (b * nt + t, 0)
    tspec = lambda width: pl.BlockSpec((tm, width), row)
    ya_spec = tspec(wa) if decode_tokens else pl.BlockSpec((tm, wa), lambda b, t: (t, b))
    weights = [w["merge"], w["w_out"], w["norm_ffn"], w["w_up"], w["w_gate"], w["conv_w"],
               w["conv_b"], w["w_down"], w["ple_norm"], w["ple_gate"], w["ple_proj"]]
    in_specs = [tspec(d), ya_spec, tspec(wr), tspec(ws), tspec(pd)]
    in_specs += [_const_spec(a.shape) for a in weights]
    args = [h2, ya, yr, ys, p2] + weights
    scratch = [pltpu.VMEM((tm + SUBLANES, FF_CHUNK), F32), pltpu.VMEM((tm, d), F32)]
    if decode_tokens:
        in_specs += [_const_spec(prev[0].shape), _const_spec(prev[1].shape)]
        args += list(prev)
        conv_rows = tm
    else:
        scratch.append(pltpu.VMEM((nchunk, SUBLANES, FF_CHUNK), F32))
        conv_rows = SUBLANES
    return pl.pallas_call(
        functools.partial(_mix_ffn_kernel, tm=tm, wa=wa, wr=wr, decode_tokens=decode_tokens),
        grid=(nb, nt),
        in_specs=in_specs,
        out_specs=[tspec(d),
                   pl.BlockSpec((None, nchunk, conv_rows, FF_CHUNK), lambda b, t: (b, 0, 0, 0))],
        out_shape=[jax.ShapeDtypeStruct((nb * seq, d), F32),
                   jax.ShapeDtypeStruct((nb, nchunk, conv_rows, FF_CHUNK), F32)],
        scratch_shapes=scratch,
        compiler_params=_params(("arbitrary", "arbitrary")),
        name="mix_ffn_step" if decode_tokens else "mix_ffn",
    )(*args)


def _block_diag(blocks):
    g, r, c = blocks.shape
    eye = jnp.eye(g, dtype=blocks.dtype)
    return (eye[:, None, :, None] * blocks[:, :, None, :]).reshape(g * r, g * c)


def _s5_params(lam_re, lam_im, log_dt, b_re, b_im, c_re, c_im):
    dt = jnp.exp(log_dt.astype(F32))[:, None]
    lr = lam_re.astype(F32)
    li = lam_im.astype(F32)
    mag = jnp.exp(lr * dt)
    ang = li * dt
    ab_re = mag * jnp.cos(ang)
    ab_im = mag * jnp.sin(ang)
    den = lr * lr + li * li
    f_re = ((ab_re - 1.0) * lr + ab_im * li) / den
    f_im = (ab_im * lr - (ab_re - 1.0) * li) / den
    bb_re = f_re[..., None] * b_re - f_im[..., None] * b_im
    bb_im = f_re[..., None] * b_im + f_im[..., None] * b_re
    to_in = lambda m: _block_diag(jnp.swapaxes(m, 1, 2))
    bblk = jnp.concatenate([to_in(bb_re), to_in(bb_im)], axis=1)
    to_out = lambda m: _block_diag(jnp.swapaxes(m, 1, 2))
    cblk = jnp.concatenate([to_out(c_re), -to_out(c_im)], axis=0)
    return ab_re.reshape(1, -1), ab_im.reshape(1, -1), bblk.astype(BF16), cblk.astype(BF16)


def _ret_consts(nh, chunk, group):
    log_g = jnp.log1p(-jnp.exp2(-5.0 - jnp.arange(nh, dtype=F32)))
    idx = jnp.arange(chunk, dtype=jnp.int32)
    pos = (idx % group).astype(F32)
    diff = pos[:, None] - pos[None, :]
    ok = (diff >= 0) & ((idx // group)[:, None] == (idx // group)[None, :])
    dmask = jnp.where(ok[None], jnp.exp(jnp.where(ok, diff, 0.0)[None] * log_g[:, None, None]), 0.0)
    xi = jnp.exp((pos + 1.0)[None, :] * log_g[:, None])
    zeta = jnp.exp((group - 1.0 - pos)[None, :] * log_g[:, None])
    d_chunk = jnp.exp(group * log_g)
    lanes = lambda t: jnp.repeat(t.T, HEAD_DIM, axis=1)
    return dmask, lanes(xi), lanes(zeta), jnp.repeat(d_chunk, HEAD_DIM)[None, :]


def _rope_tables(pos, nh):
    half = HEAD_DIM // 2
    inv = jnp.power(ROPE_BASE, -jnp.arange(half, dtype=F32) / half)
    ang = pos.astype(F32)[:, None] * inv[None, :]
    cos = jnp.cos(ang)
    sin = jnp.sin(ang)
    return (jnp.tile(jnp.concatenate([cos, cos], axis=1), (1, nh)),
            jnp.tile(jnp.concatenate([-sin, sin], axis=1), (1, nh)))


def _group_mean_matrix(width):
    g = jnp.arange(width) // HEAD_DIM
    return jnp.where(g[:, None] == g[None, :], 1.0 / HEAD_DIM, 0.0).astype(BF16)


def _suffix_matrix(n):
    i = jnp.arange(n)
    return (i[:, None] >= i[None, :]).astype(BF16)


def _chunk_cols(m):
    r, f = m.shape
    return m.reshape(r, f // FF_CHUNK, FF_CHUNK).transpose(1, 0, 2)


def _unchunk_cols(m):
    b, n, r, c = m.shape
    return m.transpose(0, 2, 1, 3).reshape(b, r, n * c)


def _layer_weights(i, wts):
    (norm_mix_gain, w_in, lam_re, lam_im, log_dt, b_re, b_im, c_re, c_im, s5_d, glu_w, glu_b,
     ret_gn, q_gain, k_gain, sb_bias, merge, w_out, norm_ffn, w_up, w_gate, conv_w, conv_b, w_down,
     ple_norm, ple_gate, ple_proj) = [t[i] for t in wts]
    nh_s = sb_bias.shape[0]
    row = lambda v: v.reshape(1, -1)
    are, aim, bblk, cblk = _s5_params(lam_re, lam_im, log_dt, b_re, b_im, c_re, c_im)
    conv_w8 = jnp.concatenate([conv_w, jnp.zeros((SUBLANES - CONV_W, conv_w.shape[1]), F32)], axis=0)
    return dict(
        norm_mix=row(norm_mix_gain), w_in=w_in.astype(BF16),
        are=are, aim=aim, bblk=bblk, cblk=cblk, d_skip=row(s5_d), glu_w=glu_w.astype(BF16),
        glu_b=row(glu_b), ret_gn=row(ret_gn),
        q_gain=row(jnp.tile(q_gain, nh_s)), k_gain=row(jnp.tile(k_gain, nh_s)), sb_bias=sb_bias,
        merge=row(merge), w_out=w_out.astype(BF16), norm_ffn=row(norm_ffn),
        w_up=_chunk_cols(w_up.astype(BF16)), w_gate=_chunk_cols(w_gate.astype(BF16)),
        conv_w=_chunk_cols(conv_w8), conv_b=_chunk_cols(row(conv_b)),
        w_down=w_down.astype(BF16).reshape(-1, FF_CHUNK, w_down.shape[1]),
        ple_norm=row(ple_norm), ple_gate=ple_gate.astype(BF16), ple_proj=ple_proj.astype(BF16))


def _prompt_layer(h2, p2, w, consts, *, nb, seq):
    wa, wr, ws, nh_r = consts["wa"], consts["wr"], consts["ws"], consts["nh_r"]
    tm, tq, chunk, tc = consts["tm"], consts["tq"], consts["chunk"], consts["tc"]
    ua, qr, kr, vr, gr, qs, ks, vs = _proj_in(
        h2, w["norm_mix"], w["w_in"], consts["cos"], consts["sin"], w["q_gain"], w["k_gain"],
        consts["gm_s"], nb=nb, seq=seq, tm=tm, wa=wa, wr=wr, ws=ws)
    zeros = jnp.zeros((nb, w["are"].shape[1]), F32)
    ya, sre, sim = _s5(ua.reshape(seq * nb, wa), zeros, zeros, w["are"], w["aim"], w["bblk"],
                       w["cblk"], w["d_skip"], w["glu_w"], w["glu_b"], nb=nb, seq=seq, tc=tc)
    yr, sret = _retention(qr, kr, vr, gr, consts["dmask"], consts["xi"], consts["zeta"],
                          consts["dcol"], w["ret_gn"], consts["gm_r"], nb=nb, seq=seq,
                          chunk=chunk, nh=nh_r)
    ys = _stick_breaking(qs, ks, vs, w["sb_bias"], consts["later_q"], nb=nb, seq=seq, tq=tq)
    h_out, conv = _mix_ffn(h2, ya.reshape(seq, nb * wa), yr, ys, p2, w, nb=nb, seq=seq, tm=tm)
    ret_state = jnp.stack([sret[:, h * HEAD_DIM:(h + 1) * HEAD_DIM, h * HEAD_DIM:(h + 1) * HEAD_DIM]
                           for h in range(nh_r)], axis=1)
    conv_state = _unchunk_cols(conv)[:, SUBLANES - (CONV_W - 1):, :]
    return h_out, (ks, vs, sre, sim, ret_state, conv_state)


def _decode_layer(h2, p2, w, consts, past, *, layer, nb, ntok):
    wa, wr, ws, nh_r, nh_s = consts["wa"], consts["wr"], consts["ws"], consts["nh_r"], consts["nh_s"]
    rows = nb * ntok
    ua, qr, kr, vr, gr, qs, ks, vs = _proj_in(
        h2, w["norm_mix"], w["w_in"], consts["cos"], consts["sin"], w["q_gain"], w["k_gain"],
        consts["gm_s"], nb=1, seq=rows, tm=rows, wa=wa, wr=wr, ws=ws)
    to_tm = lambda a: a.reshape(nb, ntok, -1).transpose(1, 0, 2).reshape(rows, -1)
    from_tm = lambda a: a.reshape(ntok, nb, -1).transpose(1, 0, 2).reshape(rows, -1)
    ya, sre, sim = _s5(to_tm(ua), past["ssm_re"], past["ssm_im"], w["are"], w["aim"], w["bblk"],
                       w["cblk"], w["d_skip"], w["glu_w"], w["glu_b"], nb=nb, seq=ntok, tc=ntok)
    s0cat = past["ret"].transpose(0, 2, 1, 3).reshape(nb * HEAD_DIM, wr)
    yr, snew = _retention_step(qr, kr, vr, gr, consts["dmask"], consts["xi"], consts["zeta"],
                               consts["dcol"], w["ret_gn"], consts["gm_r"], s0cat,
                               consts["expand"], nh=nh_r, ntok=ntok)
    ret_state = snew.reshape(nb, HEAD_DIM, nh_r, HEAD_DIM).transpose(0, 2, 1, 3)
    q_rep = jnp.repeat(qs.reshape(nb, ntok, ws), nh_s, axis=1)
    q_exp = jnp.where(consts["q_own"], q_rep, 0.0)
    pad = lambda a: jnp.pad(a.reshape(nb, ntok, ws), ((0, 0), (0, -ntok % SUBLANES), (0, 0)))
    bias_rows = jnp.tile(w["sb_bias"], ntok).reshape(ntok * nh_s, 1)
    ys = _stick_breaking_step(past["page_table"], q_exp, pad(ks), pad(vs), bias_rows,
                              consts["later_p"], past["cache_k"], past["cache_v"], layer=layer,
                              pp=consts["pp"], nh=nh_s).reshape(rows, ws)
    buf = past["conv"]
    zero = jnp.zeros_like(buf[:, :1])
    prev1 = jnp.concatenate([buf[:, 1:2]] + [zero] * (ntok - 1), axis=1).reshape(rows, -1)
    prev2 = jnp.concatenate([buf[:, 0:1], buf[:, 1:2]] + [zero] * (ntok - 2), axis=1).reshape(rows, -1)
    h_out, conv = _mix_ffn(h2, from_tm(ya), yr, ys, p2, w, (_chunk_cols(prev1), _chunk_cols(prev2)),
                           nb=1, seq=rows, tm=rows, decode_tokens=ntok)
    conv_state = _unchunk_cols(conv).reshape(nb, ntok, -1)[:, ntok - (CONV_W - 1):, :]
    return h_out, (ks, vs, sre, sim, ret_state, conv_state)


def kernel(x_prompt, x_sample, cache_k, cache_v, state_ssm_re, state_ssm_im, state_ret, state_conv, page_table, p_prompt, p_sample, norm_mix_gain, w_in, s5_lambda_re, s5_lambda_im, s5_log_dt, s5_b_re, s5_b_im, s5_c_re, s5_c_im, s5_d, s5_glu_w, s5_glu_b, ret_gn_gain, sb_q_gain, sb_k_gain, sb_logit_bias, merge_gain, w_out, norm_ffn_gain, ffn_w_up, ffn_w_gate, ffn_conv_w, ffn_conv_b, ffn_w_down, ple_norm_gain, ple_w_gate, ple_w_proj):
    wts = (norm_mix_gain, w_in, s5_lambda_re, s5_lambda_im, s5_log_dt, s5_b_re, s5_b_im, s5_c_re,
           s5_c_im, s5_d, s5_glu_w, s5_glu_b, ret_gn_gain, sb_q_gain, sb_k_gain, sb_logit_bias,
           merge_gain, w_out, norm_ffn_gain, ffn_w_up, ffn_w_gate, ffn_conv_w, ffn_conv_b,
           ffn_w_down, ple_norm_gain, ple_w_gate, ple_w_proj)
    depth = w_in.shape[0]
    nb, seq, d = x_prompt.shape
    nbs, ntok, _ = x_sample.shape
    past_len = page_table.shape[1] * PAGE_SIZE
    wa = s5_d.shape[1]
    wr = ret_gn_gain.shape[1]
    nh_r = wr // HEAD_DIM
    nh_s = sb_logit_bias.shape[1]
    ws = nh_s * HEAD_DIM
    n_pool = cache_k.shape[1]
    rows_s = nbs * ntok

    shared = dict(wa=wa, wr=wr, ws=ws, nh_r=nh_r, nh_s=nh_s, gm_s=_group_mean_matrix(ws),
                  gm_r=_group_mean_matrix(wr))
    chunk = min(128, seq)
    cos_p, sin_p = _rope_tables(jnp.arange(seq, dtype=jnp.int32), nh_r)
    dmask_p, xi_p, zeta_p, dcol_p = _ret_consts(nh_r, chunk, chunk)
    tq = min(256, seq)
    consts_p = dict(shared, cos=cos_p, sin=sin_p, dmask=dmask_p, xi=xi_p, zeta=zeta_p, dcol=dcol_p,
                    tm=min(512, seq), tq=tq, chunk=chunk, tc=min(128, seq),
                    later_q=_suffix_matrix(tq))
    pos_s = past_len + jnp.tile(jnp.arange(ntok, dtype=jnp.int32), nbs)
    cos_s, sin_s = _rope_tables(pos_s, nh_r)
    dmask_s, xi_s, zeta_s, dcol_s = _ret_consts(nh_r, rows_s, ntok)
    lane_src = jnp.arange(wr)
    lane_dst = jnp.arange(nbs * HEAD_DIM)
    expand = jnp.stack([(lane_src[:, None] == h * HEAD_DIM + lane_dst[None, :] % HEAD_DIM)
                        for h in range(nh_r)]).astype(BF16)
    q_own = (jnp.arange(ws)[None, :] // HEAD_DIM) == (jnp.arange(ntok * nh_s)[:, None] % nh_s)
    consts_s = dict(shared, cos=cos_s, sin=sin_s, dmask=dmask_s, xi=xi_s, zeta=zeta_s, dcol=dcol_s,
                    expand=expand, q_own=q_own[None], later_p=_suffix_matrix(PAGE_SIZE),
                    pp=min(PAGES_PER_STEP, page_table.shape[1]))

    ck = cache_k.transpose(0, 1, 3, 4, 2).reshape(depth, n_pool, ws, PAGE_SIZE)
    cv = cache_v.transpose(0, 1, 3, 4, 2).reshape(depth, n_pool, ws, PAGE_SIZE)
    h_p = x_prompt.reshape(nb * seq, d)
    h_s = x_sample.reshape(rows_s, d)
    outs_p, outs_s = [], []
    for i in range(depth):
        w = _layer_weights(i, wts)
        h_p, st_p = _prompt_layer(h_p, p_prompt[i].reshape(nb * seq, -1), w, consts_p, nb=nb, seq=seq)
        past = dict(page_table=page_table, cache_k=ck, cache_v=cv,
                    ssm_re=state_ssm_re[i].reshape(nbs, -1), ssm_im=state_ssm_im[i].reshape(nbs, -1),
                    ret=state_ret[i], conv=state_conv[i])
        h_s, st_s = _decode_layer(h_s, p_sample[i].reshape(rows_s, -1), w, consts_s, past,
                                  layer=i, nb=nbs, ntok=ntok)
        outs_p.append(st_p)
        outs_s.append(st_s)

    def stack(outs, b, t):
        k, v, sre, sim, ret, conv = [jnp.stack(x, axis=0) for x in zip(*outs)]
        kv_shape = (depth, b, t, nh_s, HEAD_DIM)
        st_shape = (depth, b, wa // S5_CH, S5_STATE)
        return (k.reshape(kv_shape), v.reshape(kv_shape), sre.reshape(st_shape),
                sim.reshape(st_shape), ret, conv)

    return ((h_p.reshape(nb, seq, d), h_s.reshape(nbs, ntok, d))
            + stack(outs_p, nb, seq) + stack(outs_s, nbs, ntok))
```

```python
import functools

import jax
import jax.numpy as jnp
from jax import lax
from jax.experimental import pallas as pl
from jax.experimental.pallas import tpu as pltpu

F32 = jnp.float32
BF16 = jnp.bfloat16

EPS = 1e-6
LOG2E = 1.4426950408889634
ROPE_BASE = 10000.0
HEAD_DIM = 64
S5_CH = 16
S5_STATE = 64
PAGE_SIZE = 128
CONV_W = 3
LANES = 128
SUBLANES = 8
FF_CHUNK = 256
PAGES_PER_STEP = 16
VMEM_LIMIT = 56 * 1024 * 1024

NT_DIMS = (((1,), (1,)), ((), ()))


def _dot(a, b):
    return jnp.dot(a.astype(BF16), b.astype(BF16), preferred_element_type=F32)


def _dot_nt(a, b):
    return lax.dot_general(a.astype(BF16), b.astype(BF16), NT_DIMS, preferred_element_type=F32)


def _dot_split(x, m):
    hi = x.astype(BF16)
    lo = (x - hi.astype(F32)).astype(BF16)
    return (jnp.dot(hi, m, preferred_element_type=F32)
            + jnp.dot(lo, m, preferred_element_type=F32))


def _rms(x, gain):
    return x * lax.rsqrt(jnp.mean(x * x, axis=-1, keepdims=True) + EPS) * gain


def _params(sem):
    return pltpu.CompilerParams(dimension_semantics=sem, vmem_limit_bytes=VMEM_LIMIT)


def _const_spec(shape, single_buffer=True):
    n = len(shape)
    if single_buffer:
        return pl.BlockSpec(shape, lambda *_: (0,) * n, pipeline_mode=pl.Buffered(1))
    return pl.BlockSpec(shape, lambda *_: (0,) * n)


def _proj_in_kernel(*refs, wa, wr, ws, n_prev):
    h_ref, g_ref, w_ref, cos_ref, sin_ref, qg_ref, kg_ref, gm_ref = refs[:8]
    refs = refs[8:]
    if n_prev > 0:
        kprev_ref, vprev_ref = refs[:2]
        refs = refs[2:]
    if n_prev < 0:
        ua_ref, qr_ref, kr_ref, vr_ref, gr_ref, qs_ref, ks_ref, vs_ref = refs
    else:
        ua_ref, qr_ref, kr_ref, vr_ref, gr_ref, qs_ref, vs_ref, kst_ref, vst_ref = refs
    n = _rms(h_ref[...], g_ref[...]).astype(BF16)

    def seg(lo, width):
        return jnp.dot(n, w_ref[:, lo:lo + width], preferred_element_type=F32)

    cos = cos_ref[...]
    sin = sin_ref[...]
    lane = lax.broadcasted_iota(jnp.int32, (1, wr), 1)
    first_half = (lane & (HEAD_DIM - 1)) < HEAD_DIM // 2

    def rope(v):
        swapped = jnp.where(first_half, pltpu.roll(v, wr - HEAD_DIM // 2, 1),
                            pltpu.roll(v, HEAD_DIM // 2, 1))
        return v * cos + swapped * sin

    gm = gm_ref[...]

    def head_rms(v, gain):
        ms = jnp.dot((v * v).astype(BF16), gm, preferred_element_type=F32)
        return v * lax.rsqrt(ms + EPS) * gain

    o = 0
    ua_ref[...] = seg(o, wa)
    o += wa
    qr_ref[...] = rope(seg(o, wr))
    o += wr
    kr_ref[...] = rope(seg(o, wr)) * (HEAD_DIM ** -0.5)
    o += wr
    vr_ref[...] = seg(o, wr)
    o += wr
    gr_ref[...] = seg(o, wr)
    o += wr
    qs_ref[...] = head_rms(seg(o, ws), qg_ref[...]) * (LOG2E * HEAD_DIM ** -0.5)
    o += ws
    ks = head_rms(seg(o, ws), kg_ref[...])
    o += ws
    vs = seg(o, ws)
    vs_ref[...] = vs
    if n_prev < 0:
        ks_ref[...] = ks
    else:
        if n_prev > 0:
            kst_ref[0:n_prev] = kprev_ref[...]
            vst_ref[0:n_prev] = vprev_ref[...]
        kst_ref[n_prev] = ks.T
        vst_ref[n_prev] = vs.T


def _proj_in(h2, gain, w_in, cos_t, sin_t, q_gain, k_gain, gm, kv_prev=None, *, nb, seq, tm, wa, wr,
             ws, n_prev=-1):
    d = h2.shape[1]
    nt = seq // tm
    row = lambda b, t: (b * nt + t, 0)
    tspec = lambda w: pl.BlockSpec((tm, w), row)
    args = [h2, gain, w_in, cos_t, sin_t, q_gain, k_gain, gm]
    in_specs = [tspec(d), _const_spec((1, d)), _const_spec(w_in.shape),
                pl.BlockSpec((tm, wr), lambda b, t: (t, 0)),
                pl.BlockSpec((tm, wr), lambda b, t: (t, 0)),
                _const_spec((1, ws)), _const_spec((1, ws)), _const_spec((ws, ws))]
    out_shape = [jax.ShapeDtypeStruct((seq, nb * wa), F32)]
    out_shape += [jax.ShapeDtypeStruct((nb * seq, wr), F32)] * 4
    out_shape += [jax.ShapeDtypeStruct((nb * seq, ws), F32)] * 2
    out_specs = [pl.BlockSpec((tm, wa), lambda b, t: (t, b))] + [tspec(wr)] * 4 + [tspec(ws)] * 2
    if n_prev < 0:
        out_shape += [jax.ShapeDtypeStruct((nb * seq, ws), F32)]
        out_specs += [tspec(ws)]
    else:
        stack = lambda n: pl.BlockSpec((n, ws, tm), lambda b, t: (0, b, t))
        if n_prev > 0:
            args += list(kv_prev)
            in_specs += [stack(n_prev)] * 2
        out_shape += [jax.ShapeDtypeStruct((n_prev + 1, nb * ws, seq), F32)] * 2
        out_specs += [stack(n_prev + 1)] * 2
    return pl.pallas_call(
        functools.partial(_proj_in_kernel, wa=wa, wr=wr, ws=ws, n_prev=n_prev),
        grid=(nb, nt),
        in_specs=in_specs,
        out_specs=out_specs,
        out_shape=out_shape,
        compiler_params=_params(("arbitrary", "arbitrary")),
        name="proj_in",
    )(*args)


def _s5_kernel(u_ref, s0re_ref, s0im_ref, are_ref, aim_ref, bblk_ref, cblk_ref, dsk_ref,
               gw_ref, gb_ref, y_ref, fre_ref, fim_ref, drv_scr, hre_scr, him_scr, *, nb, tc):
    ns = hre_scr.shape[1]

    @pl.when(pl.program_id(0) == 0)
    def _():
        hre_scr[...] = s0re_ref[...]
        him_scr[...] = s0im_ref[...]

    u = u_ref[...]
    drv_scr[...] = _dot(u, bblk_ref[...])
    are = jnp.broadcast_to(are_ref[...], (nb, ns))
    aim = jnp.broadcast_to(aim_ref[...], (nb, ns))

    def step(t, carry):
        hre, him = carry
        rows = pl.ds(pl.multiple_of(t * nb, nb), nb)
        nre = are * hre - aim * him + drv_scr[rows, 0:ns]
        nim = are * him + aim * hre + drv_scr[rows, ns:2 * ns]
        drv_scr[rows, 0:ns] = nre
        drv_scr[rows, ns:2 * ns] = nim
        return nre, nim

    hre, him = lax.fori_loop(0, tc, step, (hre_scr[...], him_scr[...]))
    hre_scr[...] = hre
    him_scr[...] = him
    fre_ref[...] = hre
    fim_ref[...] = him
    y = _dot(drv_scr[...], cblk_ref[...]) + dsk_ref[...] * u
    gy = jax.nn.gelu(y)
    y_ref[...] = gy * jax.nn.sigmoid(_dot(gy, gw_ref[...]) + gb_ref[...])


def _s5(u_tm, s0re, s0im, are, aim, bblk, cblk, dsk, gw, gb, *, nb, seq, tc):
    wa = u_tm.shape[1]
    ns = are.shape[1]
    rows = tc * nb
    return pl.pallas_call(
        functools.partial(_s5_kernel, nb=nb, tc=tc),
        grid=(seq // tc,),
        in_specs=[pl.BlockSpec((rows, wa), lambda i: (i, 0)),
                  _const_spec((nb, ns)), _const_spec((nb, ns)),
                  _const_spec((1, ns)), _const_spec((1, ns)),
                  _const_spec(bblk.shape), _const_spec(cblk.shape), _const_spec((1, wa)),
                  _const_spec(gw.shape), _const_spec((1, wa))],
        out_specs=[pl.BlockSpec((rows, wa), lambda i: (i, 0)),
                   pl.BlockSpec((nb, ns), lambda i: (0, 0)),
                   pl.BlockSpec((nb, ns), lambda i: (0, 0))],
        out_shape=[jax.ShapeDtypeStruct((seq * nb, wa), F32),
                   jax.ShapeDtypeStruct((nb, ns), F32),
                   jax.ShapeDtypeStruct((nb, ns), F32)],
        scratch_shapes=[pltpu.VMEM((rows, 2 * ns), F32), pltpu.VMEM((nb, ns), F32),
                        pltpu.VMEM((nb, ns), F32)],
        compiler_params=_params(("arbitrary",)),
        name="s5",
    )(u_tm, s0re, s0im, are, aim, bblk, cblk, dsk, gw, gb)


def _head_lane_mask(width, h):
    lane = lax.broadcasted_iota(jnp.int32, (1, width), 1)
    return (lane >> 6) == h


def _ret_intra(q, k, v, dmask_ref, nh):
    wr = q.shape[1]
    o = jnp.zeros(q.shape, F32)
    for h in range(nh):
        mh = _head_lane_mask(wr, h)
        s = _dot_nt(jnp.where(mh, q, 0.0), k) * dmask_ref[h]
        o = jnp.where(mh, _dot(s, v), o)
    return o


def _ret_finish(o, gate, gn_ref, gm_ref):
    gm = gm_ref[...]
    xc = o - _dot_split(o, gm)
    var = _dot_split(xc * xc, gm)
    return jax.nn.silu(gate) * (xc * lax.rsqrt(var + EPS) * gn_ref[...])


def _ret_kernel(q_ref, k_ref, v_ref, g_ref, dmask_ref, xi_ref, zeta_ref, dcol_ref, gn_ref, gm_ref,
                y_ref, sfin_ref, s_scr, *, nh):
    wr = q_ref.shape[1]

    @pl.when(pl.program_id(1) == 0)
    def _():
        s_scr[...] = jnp.zeros(s_scr.shape, F32)

    q = q_ref[...]
    k = k_ref[...]
    v = v_ref[...]
    s_prev = s_scr[...]
    o = _ret_intra(q, k, v, dmask_ref, nh) + _dot(q * xi_ref[...], s_prev)
    row_h = lax.broadcasted_iota(jnp.int32, (wr, wr), 0) >> 6
    col_h = lax.broadcasted_iota(jnp.int32, (wr, wr), 1) >> 6
    kv = _dot((k * zeta_ref[...]).T, v)
    s_new = s_prev * dcol_ref[...] + jnp.where(row_h == col_h, kv, 0.0)
    s_scr[...] = s_new
    sfin_ref[...] = s_new
    y_ref[...] = _ret_finish(o, g_ref[...], gn_ref, gm_ref)


def _retention(q, k, v, g, dmask, xi, zeta, dcol, gn, gm, *, nb, seq, chunk, nh):
    wr = q.shape[1]
    nc = seq // chunk
    tspec = pl.BlockSpec((chunk, wr), lambda b, c: (b * nc + c, 0))
    return pl.pallas_call(
        functools.partial(_ret_kernel, nh=nh),
        grid=(nb, nc),
        in_specs=[tspec] * 4 + [_const_spec(dmask.shape), _const_spec(xi.shape),
                                _const_spec(zeta.shape), _const_spec((1, wr)),
                                _const_spec((1, wr)), _const_spec((wr, wr))],
        out_specs=[tspec, pl.BlockSpec((None, wr, wr), lambda b, c: (b, 0, 0))],
        out_shape=[jax.ShapeDtypeStruct((nb * seq, wr), F32),
                   jax.ShapeDtypeStruct((nb, wr, wr), F32)],
        scratch_shapes=[pltpu.VMEM((wr, wr), F32)],
        compiler_params=_params(("arbitrary", "arbitrary")),
        name="retention",
    )(q, k, v, g, dmask, xi, zeta, dcol, gn, gm)


def _ret_step_kernel(q_ref, k_ref, v_ref, g_ref, dmask_ref, xi_ref, zeta_ref, dcol_ref, gn_ref,
                     gm_ref, s0_ref, exp_ref, y_ref, snew_ref, *, nh, ntok):
    t_rows, wr = q_ref.shape
    sw = s0_ref.shape[0]
    q = q_ref[...]
    k = k_ref[...]
    v = v_ref[...]
    s0 = s0_ref[...]
    o = _ret_intra(q, k, v, dmask_ref, nh)
    qx = q * xi_ref[...]
    kz = k * zeta_ref[...]
    own_b = ((lax.broadcasted_iota(jnp.int32, (t_rows, sw), 1) >> 6)
             == lax.broadcasted_iota(jnp.int32, (t_rows, sw), 0) // ntok)
    s_new = s0 * dcol_ref[...]
    for h in range(nh):
        mh = _head_lane_mask(wr, h)
        q_exp = jnp.where(own_b, _dot(jnp.where(mh, qx, 0.0), exp_ref[h]), 0.0)
        k_exp = jnp.where(own_b, _dot(jnp.where(mh, kz, 0.0), exp_ref[h]), 0.0)
        o = o + jnp.where(mh, _dot(q_exp, s0), 0.0)
        s_new = s_new + jnp.where(mh, _dot(k_exp.T, v), 0.0)
    snew_ref[...] = s_new
    y_ref[...] = _ret_finish(o, g_ref[...], gn_ref, gm_ref)


def _retention_step(q, k, v, g, dmask, xi, zeta, dcol, gn, gm, s0cat, expand, *, nh, ntok):
    t_rows, wr = q.shape
    args = (q, k, v, g, dmask, xi, zeta, dcol, gn, gm, s0cat, expand)
    return pl.pallas_call(
        functools.partial(_ret_step_kernel, nh=nh, ntok=ntok),
        grid=(1,),
        in_specs=[_const_spec(a.shape) for a in args],
        out_specs=[_const_spec((t_rows, wr), False), _const_spec(s0cat.shape, False)],
        out_shape=[jax.ShapeDtypeStruct((t_rows, wr), F32),
                   jax.ShapeDtypeStruct(s0cat.shape, F32)],
        compiler_params=_params(("arbitrary",)),
        name="retention_step",
    )(*args)


def _sb_fail(z2, causal):
    g = jnp.maximum(z2, 0.0) + jnp.log(1.0 + jnp.exp2(-jnp.abs(z2))) * LOG2E
    return g if causal is None else jnp.where(causal, g, 0.0)


def _sb_weight(z2, suffix, run, causal):
    w = jnp.exp2(z2 - suffix - run)
    return w if causal is None else jnp.where(causal, w, 0.0)


def _sb_kernel(bias_ref, q_ref, k_ref, v_ref, u_ref, o_ref, *, tq):
    hp = pl.program_id(1)
    qi = pl.program_id(2)
    q = q_ref[...]
    width = q.shape[1]
    nh = width // HEAD_DIM
    rows = nh * tq
    umat = u_ref[...]
    q2 = jnp.concatenate([jnp.where(_head_lane_mask(width, h), q, 0.0) for h in range(nh)],
                         axis=0).astype(BF16)
    row_h = lax.broadcasted_iota(jnp.int32, (rows, 1), 0) // tq
    bias = jnp.zeros((rows, 1), F32)
    for h in range(nh):
        bias = jnp.where(row_h == h, bias_ref[hp * nh + h] * LOG2E, bias)
    causal = (lax.broadcasted_iota(jnp.int32, (rows, tq), 1)
              < (lax.broadcasted_iota(jnp.int32, (rows, tq), 0) & (tq - 1)))

    def tiles(j, count, carry, causal_mask):
        acc, run = carry
        keys = [pl.ds(pl.multiple_of((j - i) * tq, tq), tq) for i in range(count)]
        zs = [_dot(q2, k_ref[:, ks]) + bias for ks in keys]
        suffixes = [_dot(_sb_fail(z, causal_mask), umat) for z in zs]
        for z, suffix, ks in zip(zs, suffixes, keys):
            acc = acc + _dot(_sb_weight(z, suffix, run, causal_mask), v_ref[ks, :])
            run = run + suffix[:, 0:1]
        return acc, run

    carry = (jnp.zeros((rows, width), F32), jnp.zeros((rows, 1), F32))
    carry = tiles(qi, 1, carry, causal)
    one = qi & 1
    two = (qi >> 1) & 1
    carry = lax.fori_loop(0, one, lambda n, c: tiles(qi - 1, 1, c, None), carry)
    carry = lax.fori_loop(0, two, lambda n, c: tiles(qi - 1 - one, 2, c, None), carry)
    top = qi - 1 - one - 2 * two
    acc, _ = lax.fori_loop(0, qi >> 2, lambda n, c: tiles(top - 4 * n, 4, c, None), carry)
    out = acc[0:tq]
    for h in range(1, nh):
        out = jnp.where(_head_lane_mask(width, h), acc[h * tq:(h + 1) * tq], out)
    o_ref[...] = out


def _stick_breaking(q, kt, v, bias, umat, *, layer, nb, seq, tq):
    ws = q.shape[1]
    nq = seq // tq
    nhp = ws // LANES
    qspec = pl.BlockSpec((tq, LANES), lambda b, hp, i: (b * nq + i, hp))
    kspec = pl.BlockSpec((None, LANES, seq), lambda b, hp, i: (layer, b * nhp + hp, 0))
    return pl.pallas_call(
        functools.partial(_sb_kernel, tq=tq),
        grid=(nb, nhp, nq),
        in_specs=[pl.BlockSpec(memory_space=pltpu.SMEM), qspec, kspec,
                  pl.BlockSpec((seq, LANES), lambda b, hp, i: (b, hp)), _const_spec((tq, tq))],
        out_specs=qspec,
        out_shape=jax.ShapeDtypeStruct((nb * seq, ws), F32),
        compiler_params=_params(("arbitrary", "arbitrary", "arbitrary")),
        name="stick_breaking",
    )(bias, q, kt, v, umat)


def _sb_step_kernel(pt_ref, q_ref, kn_ref, vn_ref, bias_ref, u_ref, *rest, pp, nh):
    k_pages = rest[:pp]
    v_pages = rest[pp:2 * pp]
    o_ref = rest[2 * pp]
    acc_scr, run_scr = rest[2 * pp + 1:]
    j = pl.program_id(1)
    q = q_ref[...].astype(BF16)
    rows, width = q.shape
    bias = bias_ref[...] * LOG2E
    umat = u_ref[...]

    def sweep(z_pages, causal):
        n = len(z_pages)
        z = jnp.concatenate(z_pages, axis=0) + jnp.concatenate([bias] * n, axis=0)
        suffix = _dot(_sb_fail(z, causal), umat)
        run = run_scr[...]
        later = [None] * n
        for p in reversed(range(n)):
            later[p] = run
            run = run + suffix[p * rows:(p + 1) * rows, 0:1]
        w = _sb_weight(z, suffix, jnp.concatenate(later, axis=0), causal)
        run_scr[...] = run
        return [w[p * rows:(p + 1) * rows] for p in range(n)]

    @pl.when(j == 0)
    def _():
        run_scr[...] = jnp.zeros(run_scr.shape, F32)
        pad = jnp.zeros((PAGE_SIZE - kn_ref.shape[0], width), F32)
        kn = jnp.concatenate([kn_ref[...], pad], axis=0)
        vn = jnp.concatenate([vn_ref[...], pad], axis=0)
        row_t = lax.broadcasted_iota(jnp.int32, (rows, PAGE_SIZE), 0) // nh
        col = lax.broadcasted_iota(jnp.int32, (rows, PAGE_SIZE), 1)
        (w,) = sweep([_dot_nt(q, kn)], col < row_t)
        acc_scr[...] = _dot(w, vn)

    @pl.when(j > 0)
    def _():
        kt = jnp.concatenate([k[...].astype(BF16) for k in k_pages], axis=1)
        z = jnp.dot(q, kt, preferred_element_type=F32)
        w = sweep([z[:, p * PAGE_SIZE:(p + 1) * PAGE_SIZE] for p in range(pp)], None)
        vt = jnp.concatenate([v[...].astype(BF16) for v in v_pages], axis=1)
        acc_scr[...] += _dot_nt(jnp.concatenate(w, axis=1), vt)

    @pl.when(j == pl.num_programs(1) - 1)
    def _():
        lane_h = lax.broadcasted_iota(jnp.int32, (rows, width), 1) >> 6
        row_h = lax.broadcasted_iota(jnp.int32, (rows, width), 0) % nh
        own = jnp.where(lane_h == row_h, acc_scr[...], 0.0)
        o_ref[...] = jnp.sum(own.reshape(rows // nh, nh, width), axis=1)


def _stick_breaking_step(page_table, q_exp, k_new, v_new, bias_rows, umat, cache_k, cache_v, *,
                         layer, pp, nh):
    nb, rows, width = q_exp.shape
    n_pages = page_table.shape[1]
    n_steps = n_pages // pp
    ntok = rows // nh

    def page_map(slot):
        def index(b, j, pt):
            return (layer, pt[b, n_pages - jnp.maximum(j, 1) * pp + slot], 0, 0)
        return index

    page_specs = [pl.BlockSpec((None, None, width, PAGE_SIZE), page_map(s)) for s in range(pp)]
    per_b = lambda r: pl.BlockSpec((None, r, width), lambda b, j, pt: (b, 0, 0))
    new_spec = per_b(k_new.shape[1])
    grid_spec = pltpu.PrefetchScalarGridSpec(
        num_scalar_prefetch=1,
        grid=(nb, n_steps + 1),
        in_specs=[per_b(rows), new_spec, new_spec,
                  pl.BlockSpec((rows, 1), lambda b, j, pt: (0, 0)),
                  pl.BlockSpec((PAGE_SIZE, PAGE_SIZE), lambda b, j, pt: (0, 0))]
        + page_specs + page_specs,
        out_specs=per_b(ntok),
        scratch_shapes=[pltpu.VMEM((rows, width), F32), pltpu.VMEM((rows, 1), F32)],
    )
    return pl.pallas_call(
        functools.partial(_sb_step_kernel, pp=pp, nh=nh),
        grid_spec=grid_spec,
        out_shape=jax.ShapeDtypeStruct((nb, ntok, width), F32),
        compiler_params=_params(("arbitrary", "arbitrary")),
        name="stick_breaking_step",
    )(page_table, q_exp, k_new, v_new, bias_rows, umat, *([cache_k] * pp), *([cache_v] * pp))


def _mix_ffn_kernel(*refs, tm, wa, wr, decode_tokens):
    if decode_tokens:
        (h_ref, ya_ref, yr_ref, ys_ref, p_ref, mg_ref, wout_ref, nf_ref, wup_ref, wgate_ref,
         cw_ref, cb_ref, wdown_ref, pn_ref, pgate_ref, pproj_ref, prev1_ref, prev2_ref,
         out_ref, conv_ref, ext_scr, acc_scr) = refs
    else:
        (h_ref, ya_ref, yr_ref, ys_ref, p_ref, mg_ref, wout_ref, nf_ref, wup_ref, wgate_ref,
         cw_ref, cb_ref, wdown_ref, pn_ref, pgate_ref, pproj_ref,
         out_ref, conv_ref, ext_scr, acc_scr, carry_scr) = refs

        @pl.when(pl.program_id(1) == 0)
        def _():
            carry_scr[...] = jnp.zeros(carry_scr.shape, F32)

    mg = mg_ref[...]
    a1 = wa + wr
    merged = (_dot(_rms(ya_ref[...], mg[:, 0:wa]), wout_ref[0:wa, :])
              + _dot(_rms(yr_ref[...], mg[:, wa:a1]), wout_ref[wa:a1, :])
              + _dot(_rms(ys_ref[...], mg[:, a1:]), wout_ref[a1:, :]))
    h_mid = h_ref[...] + merged
    n2 = _rms(h_mid, nf_ref[...]).astype(BF16)
    acc_scr[...] = jnp.zeros(acc_scr.shape, F32)
    if decode_tokens:
        ext_scr[0:SUBLANES, :] = jnp.zeros((SUBLANES, FF_CHUNK), F32)
        tok = lax.broadcasted_iota(jnp.int32, (tm, FF_CHUNK), 0) % decode_tokens

    for lo in range(0, wup_ref.shape[1], FF_CHUNK):
        cols = slice(lo, lo + FF_CHUNK)
        u = jnp.dot(n2, wup_ref[:, cols], preferred_element_type=F32)
        g = jnp.dot(n2, wgate_ref[:, cols], preferred_element_type=F32)
        if not decode_tokens:
            ext_scr[0:SUBLANES, :] = carry_scr[:, cols]
        ext_scr[SUBLANES:SUBLANES + tm, :] = u
        prev1 = ext_scr[pl.ds(SUBLANES - 1, tm), :]
        prev2 = ext_scr[pl.ds(SUBLANES - 2, tm), :]
        if decode_tokens:
            prev1 = jnp.where(tok >= 1, prev1, prev1_ref[:, cols])
            prev2 = jnp.where(tok >= 2, prev2, prev2_ref[:, cols])
            conv_ref[:, cols] = u
        else:
            last = u[tm - SUBLANES:tm, :]
            carry_scr[:, cols] = last
            conv_ref[:, cols] = last
        conv = cb_ref[:, cols] + cw_ref[0:1, cols] * prev2
        conv = conv + cw_ref[1:2, cols] * prev1
        conv = conv + cw_ref[2:3, cols] * u
        acc_scr[...] += _dot(jax.nn.gelu(conv) * g, wdown_ref[cols, :])
    h2 = h_mid + acc_scr[...]
    gate = jax.nn.sigmoid(_dot(_rms(h2, pn_ref[...]), pgate_ref[...]))
    out_ref[...] = h2 + _dot(p_ref[...], pproj_ref[...]) * gate


def _mix_ffn(h2, ya, yr, ys, p2, w, prev=None, *, nb, seq, tm, decode_tokens=0):
    d = h2.shape[1]
    wa, wr, ws = w["d_skip"].shape[1], yr.shape[1], ys.shape[1]
    pd = p2.shape[1]
    nt = seq // tm
    dff = w["w_up"].shape[1]
    row = lambda b, t: (b * nt + t, 0)
    tspec = lambda width: pl.BlockSpec((tm, width), row)
    ya_spec = tspec(wa) if decode_tokens else pl.BlockSpec((tm, wa), lambda b, t: (t, b))
    weights = [w["merge"], w["w_out"], w["norm_ffn"], w["w_up"], w["w_gate"], w["conv_w"],
               w["conv_b"], w["w_down"], w["ple_norm"], w["ple_gate"], w["ple_proj"]]
    in_specs = [tspec(d), ya_spec, tspec(wr), tspec(ws), tspec(pd)]
    in_specs += [_const_spec(a.shape) for a in weights]
    args = [h2, ya, yr, ys, p2] + weights
    scratch = [pltpu.VMEM((tm + SUBLANES, FF_CHUNK), F32), pltpu.VMEM((tm, d), F32)]
    if decode_tokens:
        in_specs += [_const_spec(prev[0].shape), _const_spec(prev[1].shape)]
        args += list(prev)
        conv_rows = tm
    else:
        scratch.append(pltpu.VMEM((SUBLANES, dff), F32))
        conv_rows = SUBLANES
    return pl.pallas_call(
        functools.partial(_mix_ffn_kernel, tm=tm, wa=wa, wr=wr, decode_tokens=decode_tokens),
        grid=(nb, nt),
        in_specs=in_specs,
        out_specs=[tspec(d),
                   pl.BlockSpec((None, conv_rows, dff), lambda b, t: (b, 0, 0))],
        out_shape=[jax.ShapeDtypeStruct((nb * seq, d), F32),
                   jax.ShapeDtypeStruct((nb, conv_rows, dff), F32)],
        scratch_shapes=scratch,
        compiler_params=_params(("arbitrary", "arbitrary")),
        name="mix_ffn_step" if decode_tokens else "mix_ffn",
    )(*args)


def _block_diag(blocks):
    g, r, c = blocks.shape
    eye = jnp.eye(g, dtype=blocks.dtype)
    return (eye[:, None, :, None] * blocks[:, :, None, :]).reshape(g * r, g * c)


def _s5_params(lam_re, lam_im, log_dt, b_re, b_im, c_re, c_im):
    dt = jnp.exp(log_dt.astype(F32))[:, None]
    lr = lam_re.astype(F32)
    li = lam_im.astype(F32)
    mag = jnp.exp(lr * dt)
    ang = li * dt
    ab_re = mag * jnp.cos(ang)
    ab_im = mag * jnp.sin(ang)
    den = lr * lr + li * li
    f_re = ((ab_re - 1.0) * lr + ab_im * li) / den
    f_im = (ab_im * lr - (ab_re - 1.0) * li) / den
    bb_re = f_re[..., None] * b_re - f_im[..., None] * b_im
    bb_im = f_re[..., None] * b_im + f_im[..., None] * b_re
    to_in = lambda m: _block_diag(jnp.swapaxes(m, 1, 2))
    bblk = jnp.concatenate([to_in(bb_re), to_in(bb_im)], axis=1)
    to_out = lambda m: _block_diag(jnp.swapaxes(m, 1, 2))
    cblk = jnp.concatenate([to_out(c_re), -to_out(c_im)], axis=0)
    return ab_re.reshape(1, -1), ab_im.reshape(1, -1), bblk.astype(BF16), cblk.astype(BF16)


def _ret_consts(nh, chunk, group):
    log_g = jnp.log1p(-jnp.exp2(-5.0 - jnp.arange(nh, dtype=F32)))
    idx = jnp.arange(chunk, dtype=jnp.int32)
    pos = (idx % group).astype(F32)
    diff = pos[:, None] - pos[None, :]
    ok = (diff >= 0) & ((idx // group)[:, None] == (idx // group)[None, :])
    dmask = jnp.where(ok[None], jnp.exp(jnp.where(ok, diff, 0.0)[None] * log_g[:, None, None]), 0.0)
    xi = jnp.exp((pos + 1.0)[None, :] * log_g[:, None])
    zeta = jnp.exp((group - 1.0 - pos)[None, :] * log_g[:, None])
    d_chunk = jnp.exp(group * log_g)
    lanes = lambda t: jnp.repeat(t.T, HEAD_DIM, axis=1)
    return dmask, lanes(xi), lanes(zeta), jnp.repeat(d_chunk, HEAD_DIM)[None, :]


def _rope_tables(pos, nh):
    half = HEAD_DIM // 2
    inv = jnp.power(ROPE_BASE, -jnp.arange(half, dtype=F32) / half)
    ang = pos.astype(F32)[:, None] * inv[None, :]
    cos = jnp.cos(ang)
    sin = jnp.sin(ang)
    return (jnp.tile(jnp.concatenate([cos, cos], axis=1), (1, nh)),
            jnp.tile(jnp.concatenate([-sin, sin], axis=1), (1, nh)))


def _group_mean_matrix(width):
    g = jnp.arange(width) // HEAD_DIM
    return jnp.where(g[:, None] == g[None, :], 1.0 / HEAD_DIM, 0.0).astype(BF16)


def _suffix_matrix(n):
    i = jnp.arange(n)
    return (i[:, None] >= i[None, :]).astype(BF16)


def _layer_weights(i, wts):
    (norm_mix_gain, w_in, lam_re, lam_im, log_dt, b_re, b_im, c_re, c_im, s5_d, glu_w, glu_b,
     ret_gn, q_gain, k_gain, sb_bias, merge, w_out, norm_ffn, w_up, w_gate, conv_w, conv_b, w_down,
     ple_norm, ple_gate, ple_proj) = [t[i] for t in wts]
    nh_s = sb_bias.shape[0]
    row = lambda v: v.reshape(1, -1)
    are, aim, bblk, cblk = _s5_params(lam_re, lam_im, log_dt, b_re, b_im, c_re, c_im)
    conv_w8 = jnp.concatenate([conv_w, jnp.zeros((SUBLANES - CONV_W, conv_w.shape[1]), F32)], axis=0)
    return dict(
        norm_mix=row(norm_mix_gain), w_in=w_in.astype(BF16),
        are=are, aim=aim, bblk=bblk, cblk=cblk, d_skip=row(s5_d), glu_w=glu_w.astype(BF16),
        glu_b=row(glu_b), ret_gn=row(ret_gn),
        q_gain=row(jnp.tile(q_gain, nh_s)), k_gain=row(jnp.tile(k_gain, nh_s)), sb_bias=sb_bias,
        merge=row(merge), w_out=w_out.astype(BF16), norm_ffn=row(norm_ffn),
        w_up=w_up.astype(BF16), w_gate=w_gate.astype(BF16), conv_w=conv_w8, conv_b=row(conv_b),
        w_down=w_down.astype(BF16),
        ple_norm=row(ple_norm), ple_gate=ple_gate.astype(BF16), ple_proj=ple_proj.astype(BF16))


def _prompt_layer(h2, p2, w, consts, kv_prev, *, layer, nb, seq):
    wa, wr, ws, nh_r = consts["wa"], consts["wr"], consts["ws"], consts["nh_r"]
    tm, tq, chunk, tc = consts["tm"], consts["tq"], consts["chunk"], consts["tc"]
    ua, qr, kr, vr, gr, qs, vs, kst, vst = _proj_in(
        h2, w["norm_mix"], w["w_in"], consts["cos"], consts["sin"], w["q_gain"], w["k_gain"],
        consts["gm_s"], kv_prev, nb=nb, seq=seq, tm=tm, wa=wa, wr=wr, ws=ws, n_prev=layer)
    zeros = jnp.zeros((nb, w["are"].shape[1]), F32)
    ya, sre, sim = _s5(ua.reshape(seq * nb, wa), zeros, zeros, w["are"], w["aim"], w["bblk"],
                       w["cblk"], w["d_skip"], w["glu_w"], w["glu_b"], nb=nb, seq=seq, tc=tc)
    yr, sret = _retention(qr, kr, vr, gr, consts["dmask"], consts["xi"], consts["zeta"],
                          consts["dcol"], w["ret_gn"], consts["gm_r"], nb=nb, seq=seq,
                          chunk=chunk, nh=nh_r)
    ys = _stick_breaking(qs, kst, vs, w["sb_bias"], consts["later_q"], layer=layer, nb=nb,
                         seq=seq, tq=tq)
    h_out, conv = _mix_ffn(h2, ya.reshape(seq, nb * wa), yr, ys, p2, w, nb=nb, seq=seq, tm=tm)
    ret_state = jnp.stack([sret[:, h * HEAD_DIM:(h + 1) * HEAD_DIM, h * HEAD_DIM:(h + 1) * HEAD_DIM]
                           for h in range(nh_r)], axis=1)
    conv_state = conv[:, SUBLANES - (CONV_W - 1):, :]
    return h_out, (kst, vst), (sre, sim, ret_state, conv_state)


def _decode_layer(h2, p2, w, consts, past, *, layer, nb, ntok):
    wa, wr, ws, nh_r, nh_s = consts["wa"], consts["wr"], consts["ws"], consts["nh_r"], consts["nh_s"]
    rows = nb * ntok
    ua, qr, kr, vr, gr, qs, ks, vs = _proj_in(
        h2, w["norm_mix"], w["w_in"], consts["cos"], consts["sin"], w["q_gain"], w["k_gain"],
        consts["gm_s"], nb=1, seq=rows, tm=rows, wa=wa, wr=wr, ws=ws)
    to_tm = lambda a: a.reshape(nb, ntok, -1).transpose(1, 0, 2).reshape(rows, -1)
    from_tm = lambda a: a.reshape(ntok, nb, -1).transpose(1, 0, 2).reshape(rows, -1)
    ya, sre, sim = _s5(to_tm(ua), past["ssm_re"], past["ssm_im"], w["are"], w["aim"], w["bblk"],
                       w["cblk"], w["d_skip"], w["glu_w"], w["glu_b"], nb=nb, seq=ntok, tc=ntok)
    s0cat = past["ret"].transpose(0, 2, 1, 3).reshape(nb * HEAD_DIM, wr)
    yr, snew = _retention_step(qr, kr, vr, gr, consts["dmask"], consts["xi"], consts["zeta"],
                               consts["dcol"], w["ret_gn"], consts["gm_r"], s0cat,
                               consts["expand"], nh=nh_r, ntok=ntok)
    ret_state = snew.reshape(nb, HEAD_DIM, nh_r, HEAD_DIM).transpose(0, 2, 1, 3)
    q_rep = jnp.repeat(qs.reshape(nb, ntok, ws), nh_s, axis=1)
    q_exp = jnp.where(consts["q_own"], q_rep, 0.0)
    pad = lambda a: jnp.pad(a.reshape(nb, ntok, ws), ((0, 0), (0, -ntok % SUBLANES), (0, 0)))
    bias_rows = jnp.tile(w["sb_bias"], ntok).reshape(ntok * nh_s, 1)
    ys = _stick_breaking_step(past["page_table"], q_exp, pad(ks), pad(vs), bias_rows,
                              consts["later_p"], past["cache_k"], past["cache_v"], layer=layer,
                              pp=consts["pp"], nh=nh_s).reshape(rows, ws)
    buf = past["conv"]
    zero = jnp.zeros_like(buf[:, :1])
    prev1 = jnp.concatenate([buf[:, 1:2]] + [zero] * (ntok - 1), axis=1).reshape(rows, -1)
    prev2 = jnp.concatenate([buf[:, 0:1], buf[:, 1:2]] + [zero] * (ntok - 2), axis=1).reshape(rows, -1)
    h_out, conv = _mix_ffn(h2, from_tm(ya), yr, ys, p2, w, (prev1, prev2),
                           nb=1, seq=rows, tm=rows, decode_tokens=ntok)
    conv_state = conv.reshape(nb, ntok, -1)[:, ntok - (CONV_W - 1):, :]
    return h_out, (ks, vs, sre, sim, ret_state, conv_state)


def kernel(x_prompt, x_sample, cache_k, cache_v, state_ssm_re, state_ssm_im, state_ret, state_conv, page_table, p_prompt, p_sample, norm_mix_gain, w_in, s5_lambda_re, s5_lambda_im, s5_log_dt, s5_b_re, s5_b_im, s5_c_re, s5_c_im, s5_d, s5_glu_w, s5_glu_b, ret_gn_gain, sb_q_gain, sb_k_gain, sb_logit_bias, merge_gain, w_out, norm_ffn_gain, ffn_w_up, ffn_w_gate, ffn_conv_w, ffn_conv_b, ffn_w_down, ple_norm_gain, ple_w_gate, ple_w_proj):
    wts = (norm_mix_gain, w_in, s5_lambda_re, s5_lambda_im, s5_log_dt, s5_b_re, s5_b_im, s5_c_re,
           s5_c_im, s5_d, s5_glu_w, s5_glu_b, ret_gn_gain, sb_q_gain, sb_k_gain, sb_logit_bias,
           merge_gain, w_out, norm_ffn_gain, ffn_w_up, ffn_w_gate, ffn_conv_w, ffn_conv_b,
           ffn_w_down, ple_norm_gain, ple_w_gate, ple_w_proj)
    depth = w_in.shape[0]
    nb, seq, d = x_prompt.shape
    nbs, ntok, _ = x_sample.shape
    past_len = page_table.shape[1] * PAGE_SIZE
    wa = s5_d.shape[1]
    wr = ret_gn_gain.shape[1]
    nh_r = wr // HEAD_DIM
    nh_s = sb_logit_bias.shape[1]
    ws = nh_s * HEAD_DIM
    n_pool = cache_k.shape[1]
    rows_s = nbs * ntok

    shared = dict(wa=wa, wr=wr, ws=ws, nh_r=nh_r, nh_s=nh_s, gm_s=_group_mean_matrix(ws),
                  gm_r=_group_mean_matrix(wr))
    chunk = min(128, seq)
    cos_p, sin_p = _rope_tables(jnp.arange(seq, dtype=jnp.int32), nh_r)
    dmask_p, xi_p, zeta_p, dcol_p = _ret_consts(nh_r, chunk, chunk)
    tq = min(256, seq)
    consts_p = dict(shared, cos=cos_p, sin=sin_p, dmask=dmask_p, xi=xi_p, zeta=zeta_p, dcol=dcol_p,
                    tm=min(512, seq), tq=tq, chunk=chunk, tc=min(128, seq),
                    later_q=_suffix_matrix(tq))
    pos_s = past_len + jnp.tile(jnp.arange(ntok, dtype=jnp.int32), nbs)
    cos_s, sin_s = _rope_tables(pos_s, nh_r)
    dmask_s, xi_s, zeta_s, dcol_s = _ret_consts(nh_r, rows_s, ntok)
    lane_src = jnp.arange(wr)
    lane_dst = jnp.arange(nbs * HEAD_DIM)
    expand = jnp.stack([(lane_src[:, None] == h * HEAD_DIM + lane_dst[None, :] % HEAD_DIM)
                        for h in range(nh_r)]).astype(BF16)
    q_own = (jnp.arange(ws)[None, :] // HEAD_DIM) == (jnp.arange(ntok * nh_s)[:, None] % nh_s)
    consts_s = dict(shared, cos=cos_s, sin=sin_s, dmask=dmask_s, xi=xi_s, zeta=zeta_s, dcol=dcol_s,
                    expand=expand, q_own=q_own[None], later_p=_suffix_matrix(PAGE_SIZE),
                    pp=min(PAGES_PER_STEP, page_table.shape[1]))

    ck = cache_k.transpose(0, 1, 3, 4, 2).reshape(depth, n_pool, ws, PAGE_SIZE)
    cv = cache_v.transpose(0, 1, 3, 4, 2).reshape(depth, n_pool, ws, PAGE_SIZE)
    h_p = x_prompt.reshape(nb * seq, d)
    h_s = x_sample.reshape(rows_s, d)
    outs_p, outs_s = [], []
    kv_p = None
    for i in range(depth):
        w = _layer_weights(i, wts)
        h_p, kv_p, st_p = _prompt_layer(h_p, p_prompt[i].reshape(nb * seq, -1), w, consts_p, kv_p,
                                        layer=i, nb=nb, seq=seq)
        past = dict(page_table=page_table, cache_k=ck, cache_v=cv,
                    ssm_re=state_ssm_re[i].reshape(nbs, -1), ssm_im=state_ssm_im[i].reshape(nbs, -1),
                    ret=state_ret[i], conv=state_conv[i])
        h_s, st_s = _decode_layer(h_s, p_sample[i].reshape(rows_s, -1), w, consts_s, past,
                                  layer=i, nb=nbs, ntok=ntok)
        outs_p.append(st_p)
        outs_s.append(st_s)

    st_shape = lambda b: (depth, b, wa // S5_CH, S5_STATE)
    stack = lambda outs: [jnp.stack(x, axis=0) for x in zip(*outs)]
    k_p, v_p = [t.reshape(depth, nb, nh_s, HEAD_DIM, seq).transpose(0, 1, 4, 2, 3) for t in kv_p]
    sre_p, sim_p, ret_p, conv_p = stack(outs_p)
    k_s, v_s, sre_s, sim_s, ret_s, conv_s = stack(outs_s)
    kv_shape_s = (depth, nbs, ntok, nh_s, HEAD_DIM)
    return (h_p.reshape(nb, seq, d), h_s.reshape(nbs, ntok, d),
            k_p, v_p, sre_p.reshape(st_shape(nb)), sim_p.reshape(st_shape(nb)), ret_p, conv_p,
            k_s.reshape(kv_shape_s), v_s.reshape(kv_shape_s), sre_s.reshape(st_shape(nbs)),
            sim_s.reshape(st_shape(nbs)), ret_s, conv_s)
```

```python
import functools

import jax
import jax.numpy as jnp
from jax import lax
from jax.experimental import pallas as pl
from jax.experimental.pallas import tpu as pltpu

F32 = jnp.float32
BF16 = jnp.bfloat16

EPS = 1e-6
LOG2E = 1.4426950408889634
ROPE_BASE = 10000.0
HEAD_DIM = 64
S5_CH = 16
S5_STATE = 64
PAGE_SIZE = 128
CONV_W = 3
LANES = 128
SUBLANES = 8
FF_CHUNK = 256
DOWN_GROUP = 4
RET_SEQS = 4
PAGES_PER_STEP = 16
VMEM_LIMIT = 56 * 1024 * 1024

NT_DIMS = (((1,), (1,)), ((), ()))


def _dot(a, b):
    return jnp.dot(a.astype(BF16), b.astype(BF16), preferred_element_type=F32)


def _dot_nt(a, b):
    return lax.dot_general(a.astype(BF16), b.astype(BF16), NT_DIMS, preferred_element_type=F32)


def _dot_split(x, m):
    hi = x.astype(BF16)
    lo = (x - hi.astype(F32)).astype(BF16)
    return (jnp.dot(hi, m, preferred_element_type=F32)
            + jnp.dot(lo, m, preferred_element_type=F32))


def _rms(x, gain):
    return x * lax.rsqrt(jnp.mean(x * x, axis=-1, keepdims=True) + EPS) * gain


def _params(sem):
    return pltpu.CompilerParams(dimension_semantics=sem, vmem_limit_bytes=VMEM_LIMIT)


def _const_spec(shape, single_buffer=True):
    n = len(shape)
    if single_buffer:
        return pl.BlockSpec(shape, lambda *_: (0,) * n, pipeline_mode=pl.Buffered(1))
    return pl.BlockSpec(shape, lambda *_: (0,) * n)


def _proj_in_kernel(*refs, wa, wr, ws, n_prev):
    h_ref, g_ref, w_ref, cos_ref, sin_ref, qg_ref, kg_ref, gm_ref = refs[:8]
    refs = refs[8:]
    if n_prev > 0:
        kprev_ref, vprev_ref = refs[:2]
        refs = refs[2:]
    if n_prev < 0:
        ua_ref, qr_ref, kr_ref, vr_ref, gr_ref, qs_ref, ks_ref, vs_ref = refs
    else:
        ua_ref, qr_ref, kr_ref, vr_ref, gr_ref, qs_ref, vs_ref, kst_ref, vst_ref = refs
    n = _rms(h_ref[...], g_ref[...]).astype(BF16)

    def seg(lo, width):
        return jnp.dot(n, w_ref[:, lo:lo + width], preferred_element_type=F32)

    cos = cos_ref[...]
    sin = sin_ref[...]
    lane = lax.broadcasted_iota(jnp.int32, (1, wr), 1)
    first_half = (lane & (HEAD_DIM - 1)) < HEAD_DIM // 2

    def rope(v):
        swapped = jnp.where(first_half, pltpu.roll(v, wr - HEAD_DIM // 2, 1),
                            pltpu.roll(v, HEAD_DIM // 2, 1))
        return v * cos + swapped * sin

    gm = gm_ref[...]

    def head_rms(v, gain):
        ms = jnp.dot((v * v).astype(BF16), gm, preferred_element_type=F32)
        return v * lax.rsqrt(ms + EPS) * gain

    o = 0
    ua_ref[...] = seg(o, wa)
    o += wa
    qr_ref[...] = rope(seg(o, wr))
    o += wr
    kr_ref[...] = rope(seg(o, wr)) * (HEAD_DIM ** -0.5)
    o += wr
    vr_ref[...] = seg(o, wr)
    o += wr
    gr_ref[...] = seg(o, wr)
    o += wr
    qs_ref[...] = head_rms(seg(o, ws), qg_ref[...]) * (LOG2E * HEAD_DIM ** -0.5)
    o += ws
    ks = head_rms(seg(o, ws), kg_ref[...])
    o += ws
    vs = seg(o, ws)
    vs_ref[...] = vs
    if n_prev < 0:
        ks_ref[...] = ks
    else:
        if n_prev > 0:
            kst_ref[0:n_prev] = kprev_ref[...]
            vst_ref[0:n_prev] = vprev_ref[...]
        kst_ref[n_prev] = ks.T
        vst_ref[n_prev] = vs.T


def _proj_in(h2, gain, w_in, cos_t, sin_t, q_gain, k_gain, gm, kv_prev=None, *, nb, seq, tm, wa, wr,
             ws, n_prev=-1):
    d = h2.shape[1]
    nt = seq // tm
    row = lambda b, t: (b * nt + t, 0)
    tspec = lambda w: pl.BlockSpec((tm, w), row)
    args = [h2, gain, w_in, cos_t, sin_t, q_gain, k_gain, gm]
    in_specs = [tspec(d), _const_spec((1, d)), _const_spec(w_in.shape),
                pl.BlockSpec((tm, wr), lambda b, t: (t, 0)),
                pl.BlockSpec((tm, wr), lambda b, t: (t, 0)),
                _const_spec((1, ws)), _const_spec((1, ws)), _const_spec((ws, ws))]
    out_shape = [jax.ShapeDtypeStruct((seq, nb * wa), F32)]
    out_shape += [jax.ShapeDtypeStruct((nb * seq, wr), F32)] * 4
    out_shape += [jax.ShapeDtypeStruct((nb * seq, ws), F32)] * 2
    out_specs = [pl.BlockSpec((tm, wa), lambda b, t: (t, b))] + [tspec(wr)] * 4 + [tspec(ws)] * 2
    if n_prev < 0:
        out_shape += [jax.ShapeDtypeStruct((nb * seq, ws), F32)]
        out_specs += [tspec(ws)]
    else:
        stack = lambda n: pl.BlockSpec((n, ws, tm), lambda b, t: (0, b, t))
        if n_prev > 0:
            args += list(kv_prev)
            in_specs += [stack(n_prev)] * 2
        out_shape += [jax.ShapeDtypeStruct((n_prev + 1, nb * ws, seq), F32)] * 2
        out_specs += [stack(n_prev + 1)] * 2
    return pl.pallas_call(
        functools.partial(_proj_in_kernel, wa=wa, wr=wr, ws=ws, n_prev=n_prev),
        grid=(nb, nt),
        in_specs=in_specs,
        out_specs=out_specs,
        out_shape=out_shape,
        compiler_params=_params(("arbitrary", "arbitrary")),
        name="proj_in",
    )(*args)


def _s5_kernel(u_ref, s0re_ref, s0im_ref, are_ref, aim_ref, bblk_ref, cblk_ref, dsk_ref,
               gw_ref, gb_ref, y_ref, fre_ref, fim_ref, drv_scr, hre_scr, him_scr, *, nb, tc):
    ns = hre_scr.shape[1]

    @pl.when(pl.program_id(0) == 0)
    def _():
        hre_scr[...] = s0re_ref[...]
        him_scr[...] = s0im_ref[...]

    u = u_ref[...]
    drv_scr[...] = _dot(u, bblk_ref[...])
    are = jnp.broadcast_to(are_ref[...], (nb, ns))
    aim = jnp.broadcast_to(aim_ref[...], (nb, ns))

    def step(t, carry):
        hre, him = carry
        rows = pl.ds(pl.multiple_of(t * nb, nb), nb)
        nre = are * hre - aim * him + drv_scr[rows, 0:ns]
        nim = are * him + aim * hre + drv_scr[rows, ns:2 * ns]
        drv_scr[rows, 0:ns] = nre
        drv_scr[rows, ns:2 * ns] = nim
        return nre, nim

    hre, him = lax.fori_loop(0, tc, step, (hre_scr[...], him_scr[...]))
    hre_scr[...] = hre
    him_scr[...] = him
    fre_ref[...] = hre
    fim_ref[...] = him
    y = _dot(drv_scr[...], cblk_ref[...]) + dsk_ref[...] * u
    gy = jax.nn.gelu(y)
    y_ref[...] = gy * jax.nn.sigmoid(_dot(gy, gw_ref[...]) + gb_ref[...])


def _s5(u_tm, s0re, s0im, are, aim, bblk, cblk, dsk, gw, gb, *, nb, seq, tc):
    wa = u_tm.shape[1]
    ns = are.shape[1]
    rows = tc * nb
    return pl.pallas_call(
        functools.partial(_s5_kernel, nb=nb, tc=tc),
        grid=(seq // tc,),
        in_specs=[pl.BlockSpec((rows, wa), lambda i: (i, 0)),
                  _const_spec((nb, ns)), _const_spec((nb, ns)),
                  _const_spec((1, ns)), _const_spec((1, ns)),
                  _const_spec(bblk.shape), _const_spec(cblk.shape), _const_spec((1, wa)),
                  _const_spec(gw.shape), _const_spec((1, wa))],
        out_specs=[pl.BlockSpec((rows, wa), lambda i: (i, 0)),
                   pl.BlockSpec((nb, ns), lambda i: (0, 0)),
                   pl.BlockSpec((nb, ns), lambda i: (0, 0))],
        out_shape=[jax.ShapeDtypeStruct((seq * nb, wa), F32),
                   jax.ShapeDtypeStruct((nb, ns), F32),
                   jax.ShapeDtypeStruct((nb, ns), F32)],
        scratch_shapes=[pltpu.VMEM((rows, 2 * ns), F32), pltpu.VMEM((nb, ns), F32),
                        pltpu.VMEM((nb, ns), F32)],
        compiler_params=_params(("arbitrary",)),
        name="s5",
    )(u_tm, s0re, s0im, are, aim, bblk, cblk, dsk, gw, gb)


def _head_lane_mask(width, h):
    lane = lax.broadcasted_iota(jnp.int32, (1, width), 1)
    return (lane >> 6) == h


def _ret_intra(q, k, v, dmask_ref, nh):
    wr = q.shape[1]
    o = jnp.zeros(q.shape, F32)
    for h in range(nh):
        mh = _head_lane_mask(wr, h)
        s = _dot_nt(jnp.where(mh, q, 0.0), k) * dmask_ref[h]
        o = jnp.where(mh, _dot(s, v), o)
    return o


def _ret_finish(o, gate, gn_ref, gm_ref):
    gm = gm_ref[...]
    xc = o - _dot_split(o, gm)
    var = _dot_split(xc * xc, gm)
    return jax.nn.silu(gate) * (xc * lax.rsqrt(var + EPS) * gn_ref[...])


def _ret_kernel(q_ref, k_ref, v_ref, g_ref, dmask_ref, xi_ref, zeta_ref, dcol_ref, gn_ref, gm_ref,
                y_ref, sfin_ref, s_scr, *, nh):
    nseq, _, wr = q_ref.shape

    @pl.when(pl.program_id(1) == 0)
    def _():
        s_scr[...] = jnp.zeros(s_scr.shape, F32)

    row_h = lax.broadcasted_iota(jnp.int32, (wr, wr), 0) >> 6
    col_h = lax.broadcasted_iota(jnp.int32, (wr, wr), 1) >> 6
    for i in range(nseq):
        q = q_ref[i]
        k = k_ref[i]
        v = v_ref[i]
        s_prev = s_scr[i]
        o = _ret_intra(q, k, v, dmask_ref, nh) + _dot(q * xi_ref[...], s_prev)
        kv = _dot((k * zeta_ref[...]).T, v)
        s_new = s_prev * dcol_ref[...] + jnp.where(row_h == col_h, kv, 0.0)
        s_scr[i] = s_new
        sfin_ref[i] = s_new
        y_ref[i] = _ret_finish(o, g_ref[i], gn_ref, gm_ref)


def _retention(q, k, v, g, dmask, xi, zeta, dcol, gn, gm, *, nb, seq, chunk, nh, nseq):
    wr = q.shape[-1]
    nc = seq // chunk
    tspec = pl.BlockSpec((nseq, chunk, wr), lambda b, c: (b, c, 0))
    return pl.pallas_call(
        functools.partial(_ret_kernel, nh=nh),
        grid=(nb // nseq, nc),
        in_specs=[tspec] * 4 + [_const_spec(dmask.shape), _const_spec(xi.shape),
                                _const_spec(zeta.shape), _const_spec((1, wr)),
                                _const_spec((1, wr)), _const_spec((wr, wr))],
        out_specs=[tspec, pl.BlockSpec((nseq, wr, wr), lambda b, c: (b, 0, 0))],
        out_shape=[jax.ShapeDtypeStruct((nb, seq, wr), F32),
                   jax.ShapeDtypeStruct((nb, wr, wr), F32)],
        scratch_shapes=[pltpu.VMEM((nseq, wr, wr), F32)],
        compiler_params=_params(("arbitrary", "arbitrary")),
        name="retention",
    )(q, k, v, g, dmask, xi, zeta, dcol, gn, gm)


def _ret_step_kernel(q_ref, k_ref, v_ref, g_ref, dmask_ref, xi_ref, zeta_ref, dcol_ref, gn_ref,
                     gm_ref, s0_ref, exp_ref, y_ref, snew_ref, *, nh, ntok):
    t_rows, wr = q_ref.shape
    sw = s0_ref.shape[0]
    q = q_ref[...]
    k = k_ref[...]
    v = v_ref[...]
    s0 = s0_ref[...]
    o = _ret_intra(q, k, v, dmask_ref, nh)
    qx = q * xi_ref[...]
    kz = k * zeta_ref[...]
    own_b = ((lax.broadcasted_iota(jnp.int32, (t_rows, sw), 1) >> 6)
             == lax.broadcasted_iota(jnp.int32, (t_rows, sw), 0) // ntok)
    s_new = s0 * dcol_ref[...]
    for h in range(nh):
        mh = _head_lane_mask(wr, h)
        q_exp = jnp.where(own_b, _dot(jnp.where(mh, qx, 0.0), exp_ref[h]), 0.0)
        k_exp = jnp.where(own_b, _dot(jnp.where(mh, kz, 0.0), exp_ref[h]), 0.0)
        o = o + jnp.where(mh, _dot(q_exp, s0), 0.0)
        s_new = s_new + jnp.where(mh, _dot(k_exp.T, v), 0.0)
    snew_ref[...] = s_new
    y_ref[...] = _ret_finish(o, g_ref[...], gn_ref, gm_ref)


def _retention_step(q, k, v, g, dmask, xi, zeta, dcol, gn, gm, s0cat, expand, *, nh, ntok):
    t_rows, wr = q.shape
    args = (q, k, v, g, dmask, xi, zeta, dcol, gn, gm, s0cat, expand)
    return pl.pallas_call(
        functools.partial(_ret_step_kernel, nh=nh, ntok=ntok),
        grid=(1,),
        in_specs=[_const_spec(a.shape) for a in args],
        out_specs=[_const_spec((t_rows, wr), False), _const_spec(s0cat.shape, False)],
        out_shape=[jax.ShapeDtypeStruct((t_rows, wr), F32),
                   jax.ShapeDtypeStruct(s0cat.shape, F32)],
        compiler_params=_params(("arbitrary",)),
        name="retention_step",
    )(*args)


def _sb_fail(z2, causal):
    g = jnp.maximum(z2, 0.0) + jnp.log(1.0 + jnp.exp2(-jnp.abs(z2))) * LOG2E
    return g if causal is None else jnp.where(causal, g, 0.0)


def _sb_weight(z2, suffix, run, causal):
    w = jnp.exp2(z2 - suffix - run)
    return w if causal is None else jnp.where(causal, w, 0.0)


def _sb_kernel(bias_ref, q_ref, k_ref, v_ref, u_ref, o_ref, *, tq):
    hp = pl.program_id(1)
    qi = pl.program_id(2)
    q = q_ref[...]
    width = q.shape[1]
    nh = width // HEAD_DIM
    rows = nh * tq
    umat = u_ref[...]
    q2 = jnp.concatenate([jnp.where(_head_lane_mask(width, h), q, 0.0) for h in range(nh)],
                         axis=0).astype(BF16)
    row_h = lax.broadcasted_iota(jnp.int32, (rows, 1), 0) // tq
    bias = jnp.zeros((rows, 1), F32)
    for h in range(nh):
        bias = jnp.where(row_h == h, bias_ref[hp * nh + h] * LOG2E, bias)
    lane = lax.broadcasted_iota(jnp.int32, (rows, LANES), 1)
    piece = jnp.zeros((rows, LANES), F32)
    left = bias
    for i in range(3):
        part = left.astype(BF16).astype(F32)
        piece = jnp.where(lane == i, part, piece)
        left = left - part
    q2 = jnp.concatenate([q2, piece.astype(BF16)], axis=1)
    ones = (lax.broadcasted_iota(jnp.int32, (LANES, tq), 0) < 3).astype(BF16)
    causal = (lax.broadcasted_iota(jnp.int32, (rows, tq), 1)
              < (lax.broadcasted_iota(jnp.int32, (rows, tq), 0) & (tq - 1)))

    def tiles(j, count, carry, causal_mask):
        acc, run = carry
        keys = [pl.ds(pl.multiple_of((j - i) * tq, tq), tq) for i in range(count)]
        zs = [jnp.dot(q2, jnp.concatenate([k_ref[:, ks].astype(BF16), ones], axis=0),
                      preferred_element_type=F32) for ks in keys]
        suffixes = [_dot(_sb_fail(z, causal_mask), umat) for z in zs]
        for z, suffix, ks in zip(zs, suffixes, keys):
            acc = acc + _dot(_sb_weight(z, suffix, run, causal_mask), v_ref[ks, :])
            run = run + suffix[:, 0:1]
        return acc, run

    carry = (jnp.zeros((rows, width), F32), jnp.zeros((rows, 1), F32))
    carry = tiles(qi, 1, carry, causal)
    top = qi - 1
    for bit, count in ((0, 1), (1, 2), (2, 4)):
        has = (qi >> bit) & 1
        carry = lax.fori_loop(0, has, lambda n, c, top=top, count=count: tiles(top, count, c, None),
                              carry)
        top = top - has * count
    acc, _ = lax.fori_loop(0, qi >> 3, lambda n, c: tiles(top - 8 * n, 8, c, None), carry)
    out = acc[0:tq]
    for h in range(1, nh):
        out = jnp.where(_head_lane_mask(width, h), acc[h * tq:(h + 1) * tq], out)
    o_ref[...] = out


def _stick_breaking(q, kt, v, bias, umat, *, layer, nb, seq, tq):
    ws = q.shape[1]
    nq = seq // tq
    nhp = ws // LANES
    qspec = pl.BlockSpec((tq, LANES), lambda b, hp, i: (b * nq + i, hp))
    kspec = pl.BlockSpec((None, LANES, seq), lambda b, hp, i: (layer, b * nhp + hp, 0))
    return pl.pallas_call(
        functools.partial(_sb_kernel, tq=tq),
        grid=(nb, nhp, nq),
        in_specs=[pl.BlockSpec(memory_space=pltpu.SMEM), qspec, kspec,
                  pl.BlockSpec((seq, LANES), lambda b, hp, i: (b, hp)), _const_spec((tq, tq))],
        out_specs=qspec,
        out_shape=jax.ShapeDtypeStruct((nb * seq, ws), F32),
        compiler_params=_params(("arbitrary", "arbitrary", "arbitrary")),
        name="stick_breaking",
    )(bias, q, kt, v, umat)


def _sb_step_kernel(pt_ref, q_ref, kn_ref, vn_ref, bias_ref, u_ref, *rest, pp, nh):
    k_pages = rest[:pp]
    v_pages = rest[pp:2 * pp]
    o_ref = rest[2 * pp]
    acc_scr, run_scr = rest[2 * pp + 1:]
    j = pl.program_id(1)
    q = q_ref[...].astype(BF16)
    rows, width = q.shape
    bias = bias_ref[...] * LOG2E
    umat = u_ref[...]

    def sweep(z_pages, causal):
        n = len(z_pages)
        z = jnp.concatenate(z_pages, axis=0) + jnp.concatenate([bias] * n, axis=0)
        suffix = _dot(_sb_fail(z, causal), umat)
        run = run_scr[...]
        later = [None] * n
        for p in reversed(range(n)):
            later[p] = run
            run = run + suffix[p * rows:(p + 1) * rows, 0:1]
        w = _sb_weight(z, suffix, jnp.concatenate(later, axis=0), causal)
        run_scr[...] = run
        return [w[p * rows:(p + 1) * rows] for p in range(n)]

    @pl.when(j == 0)
    def _():
        run_scr[...] = jnp.zeros(run_scr.shape, F32)
        pad = jnp.zeros((PAGE_SIZE - kn_ref.shape[0], width), F32)
        kn = jnp.concatenate([kn_ref[...], pad], axis=0)
        vn = jnp.concatenate([vn_ref[...], pad], axis=0)
        row_t = lax.broadcasted_iota(jnp.int32, (rows, PAGE_SIZE), 0) // nh
        col = lax.broadcasted_iota(jnp.int32, (rows, PAGE_SIZE), 1)
        (w,) = sweep([_dot_nt(q, kn)], col < row_t)
        acc_scr[...] = _dot(w, vn)

    @pl.when(j > 0)
    def _():
        kt = jnp.concatenate([k[...].astype(BF16) for k in k_pages], axis=1)
        z = jnp.dot(q, kt, preferred_element_type=F32)
        w = sweep([z[:, p * PAGE_SIZE:(p + 1) * PAGE_SIZE] for p in range(pp)], None)
        vt = jnp.concatenate([v[...].astype(BF16) for v in v_pages], axis=1)
        acc_scr[...] += _dot_nt(jnp.concatenate(w, axis=1), vt)

    @pl.when(j == pl.num_programs(1) - 1)
    def _():
        lane_h = lax.broadcasted_iota(jnp.int32, (rows, width), 1) >> 6
        row_h = lax.broadcasted_iota(jnp.int32, (rows, width), 0) % nh
        own = jnp.where(lane_h == row_h, acc_scr[...], 0.0)
        o_ref[...] = jnp.sum(own.reshape(rows // nh, nh, width), axis=1)


def _stick_breaking_step(page_table, q_exp, k_new, v_new, bias_rows, umat, cache_k, cache_v, *,
                         layer, pp, nh):
    nb, rows, width = q_exp.shape
    n_pages = page_table.shape[1]
    n_steps = n_pages // pp
    ntok = rows // nh

    def page_map(slot):
        def index(b, j, pt):
            return (layer, pt[b, n_pages - jnp.maximum(j, 1) * pp + slot], 0, 0)
        return index

    page_specs = [pl.BlockSpec((None, None, width, PAGE_SIZE), page_map(s)) for s in range(pp)]
    per_b = lambda r: pl.BlockSpec((None, r, width), lambda b, j, pt: (b, 0, 0))
    new_spec = per_b(k_new.shape[1])
    grid_spec = pltpu.PrefetchScalarGridSpec(
        num_scalar_prefetch=1,
        grid=(nb, n_steps + 1),
        in_specs=[per_b(rows), new_spec, new_spec,
                  pl.BlockSpec((rows, 1), lambda b, j, pt: (0, 0)),
                  pl.BlockSpec((PAGE_SIZE, PAGE_SIZE), lambda b, j, pt: (0, 0))]
        + page_specs + page_specs,
        out_specs=per_b(ntok),
        scratch_shapes=[pltpu.VMEM((rows, width), F32), pltpu.VMEM((rows, 1), F32)],
    )
    return pl.pallas_call(
        functools.partial(_sb_step_kernel, pp=pp, nh=nh),
        grid_spec=grid_spec,
        out_shape=jax.ShapeDtypeStruct((nb, ntok, width), F32),
        compiler_params=_params(("arbitrary", "arbitrary")),
        name="stick_breaking_step",
    )(page_table, q_exp, k_new, v_new, bias_rows, umat, *([cache_k] * pp), *([cache_v] * pp))


def _mix_ffn_kernel(*refs, tm, wa, wr, decode_tokens):
    if decode_tokens:
        (h_ref, ya_ref, yr_ref, ys_ref, p_ref, mg_ref, wout_ref, nf_ref, wup_ref, wgate_ref,
         cw_ref, cb_ref, wdown_ref, pn_ref, pgate_ref, pproj_ref, prev1_ref, prev2_ref,
         out_ref, conv_ref, ext_scr, acc_scr) = refs
    else:
        (h_ref, ya_ref, yr_ref, ys_ref, p_ref, mg_ref, wout_ref, nf_ref, wup_ref, wgate_ref,
         cw_ref, cb_ref, wdown_ref, pn_ref, pgate_ref, pproj_ref,
         out_ref, conv_ref, ext_scr, acc_scr, carry_scr) = refs

        @pl.when(pl.program_id(1) == 0)
        def _():
            carry_scr[...] = jnp.zeros(carry_scr.shape, F32)

    mg = mg_ref[...]
    a1 = wa + wr
    merged = (_dot(_rms(ya_ref[...], mg[:, 0:wa]), wout_ref[0:wa, :])
              + _dot(_rms(yr_ref[...], mg[:, wa:a1]), wout_ref[wa:a1, :])
              + _dot(_rms(ys_ref[...], mg[:, a1:]), wout_ref[a1:, :]))
    h_mid = h_ref[...] + merged
    n2 = _rms(h_mid, nf_ref[...]).astype(BF16)
    acc_scr[...] = jnp.zeros(acc_scr.shape, F32)
    if decode_tokens:
        ext_scr[0:SUBLANES, :] = jnp.zeros((SUBLANES, FF_CHUNK), F32)
        tok = lax.broadcasted_iota(jnp.int32, (tm, FF_CHUNK), 0) % decode_tokens

    acts = []
    nchunk = wup_ref.shape[1] // FF_CHUNK
    ngroup = -(-nchunk // DOWN_GROUP)
    group_ends = {(i + 1) * nchunk // ngroup for i in range(ngroup)}
    for lo in range(0, wup_ref.shape[1], FF_CHUNK):
        cols = slice(lo, lo + FF_CHUNK)
        u = jnp.dot(n2, wup_ref[:, cols], preferred_element_type=F32)
        g = jnp.dot(n2, wgate_ref[:, cols], preferred_element_type=F32)
        if not decode_tokens:
            ext_scr[0:SUBLANES, :] = carry_scr[:, cols]
        ext_scr[SUBLANES:SUBLANES + tm, :] = u
        prev1 = ext_scr[pl.ds(SUBLANES - 1, tm), :]
        prev2 = ext_scr[pl.ds(SUBLANES - 2, tm), :]
        if decode_tokens:
            prev1 = jnp.where(tok >= 1, prev1, prev1_ref[:, cols])
            prev2 = jnp.where(tok >= 2, prev2, prev2_ref[:, cols])
            conv_ref[:, cols] = u
        else:
            last = u[tm - SUBLANES:tm, :]
            carry_scr[:, cols] = last
            conv_ref[:, cols] = last
        conv = cb_ref[:, cols] + cw_ref[0:1, cols] * prev2
        conv = conv + cw_ref[1:2, cols] * prev1
        conv = conv + cw_ref[2:3, cols] * u
        acts.append((jax.nn.gelu(conv) * g).astype(BF16))
        if (lo + FF_CHUNK) // FF_CHUNK in group_ends:
            width = len(acts) * FF_CHUNK
            acc_scr[...] += jnp.dot(jnp.concatenate(acts, axis=1),
                                    wdown_ref[lo + FF_CHUNK - width:lo + FF_CHUNK, :],
                                    preferred_element_type=F32)
            acts = []
    h2 = h_mid + acc_scr[...]
    gate = jax.nn.sigmoid(_dot(_rms(h2, pn_ref[...]), pgate_ref[...]))
    out_ref[...] = h2 + _dot(p_ref[...], pproj_ref[...]) * gate


def _mix_ffn(h2, ya, yr, ys, p2, w, prev=None, *, nb, seq, tm, decode_tokens=0):
    d = h2.shape[1]
    wa, wr, ws = w["d_skip"].shape[1], yr.shape[1], ys.shape[1]
    pd = p2.shape[1]
    nt = seq // tm
    dff = w["w_up"].shape[1]
    row = lambda b, t: (b * nt + t, 0)
    tspec = lambda width: pl.BlockSpec((tm, width), row)
    ya_spec = tspec(wa) if decode_tokens else pl.BlockSpec((tm, wa), lambda b, t: (t, b))
    weights = [w["merge"], w["w_out"], w["norm_ffn"], w["w_up"], w["w_gate"], w["conv_w"],
               w["conv_b"], w["w_down"], w["ple_norm"], w["ple_gate"], w["ple_proj"]]
    in_specs = [tspec(d), ya_spec, tspec(wr), tspec(ws), tspec(pd)]
    in_specs += [_const_spec(a.shape) for a in weights]
    args = [h2, ya, yr, ys, p2] + weights
    scratch = [pltpu.VMEM((tm + SUBLANES, FF_CHUNK), F32), pltpu.VMEM((tm, d), F32)]
    if decode_tokens:
        in_specs += [_const_spec(prev[0].shape), _const_spec(prev[1].shape)]
        args += list(prev)
        conv_rows = tm
    else:
        scratch.append(pltpu.VMEM((SUBLANES, dff), F32))
        conv_rows = SUBLANES
    return pl.pallas_call(
        functools.partial(_mix_ffn_kernel, tm=tm, wa=wa, wr=wr, decode_tokens=decode_tokens),
        grid=(nb, nt),
        in_specs=in_specs,
        out_specs=[tspec(d),
                   pl.BlockSpec((None, conv_rows, dff), lambda b, t: (b, 0, 0))],
        out_shape=[jax.ShapeDtypeStruct((nb * seq, d), F32),
                   jax.ShapeDtypeStruct((nb, conv_rows, dff), F32)],
        scratch_shapes=scratch,
        compiler_params=_params(("arbitrary", "arbitrary")),
        name="mix_ffn_step" if decode_tokens else "mix_ffn",
    )(*args)


def _block_diag(blocks):
    g, r, c = blocks.shape
    eye = jnp.eye(g, dtype=blocks.dtype)
    return (eye[:, None, :, None] * blocks[:, :, None, :]).reshape(g * r, g * c)


def _s5_params(lam_re, lam_im, log_dt, b_re, b_im, c_re, c_im):
    dt = jnp.exp(log_dt.astype(F32))[:, None]
    lr = lam_re.astype(F32)
    li = lam_im.astype(F32)
    mag = jnp.exp(lr * dt)
    ang = li * dt
    ab_re = mag * jnp.cos(ang)
    ab_im = mag * jnp.sin(ang)
    den = lr * lr + li * li
    f_re = ((ab_re - 1.0) * lr + ab_im * li) / den
    f_im = (ab_im * lr - (ab_re - 1.0) * li) / den
    bb_re = f_re[..., None] * b_re - f_im[..., None] * b_im
    bb_im = f_re[..., None] * b_im + f_im[..., None] * b_re
    to_in = lambda m: _block_diag(jnp.swapaxes(m, 1, 2))
    bblk = jnp.concatenate([to_in(bb_re), to_in(bb_im)], axis=1)
    to_out = lambda m: _block_diag(jnp.swapaxes(m, 1, 2))
    cblk = jnp.concatenate([to_out(c_re), -to_out(c_im)], axis=0)
    return ab_re.reshape(1, -1), ab_im.reshape(1, -1), bblk.astype(BF16), cblk.astype(BF16)


def _ret_consts(nh, chunk, group):
    log_g = jnp.log1p(-jnp.exp2(-5.0 - jnp.arange(nh, dtype=F32)))
    idx = jnp.arange(chunk, dtype=jnp.int32)
    pos = (idx % group).astype(F32)
    diff = pos[:, None] - pos[None, :]
    ok = (diff >= 0) & ((idx // group)[:, None] == (idx // group)[None, :])
    dmask = jnp.where(ok[None], jnp.exp(jnp.where(ok, diff, 0.0)[None] * log_g[:, None, None]), 0.0)
    xi = jnp.exp((pos + 1.0)[None, :] * log_g[:, None])
    zeta = jnp.exp((group - 1.0 - pos)[None, :] * log_g[:, None])
    d_chunk = jnp.exp(group * log_g)
    lanes = lambda t: jnp.repeat(t.T, HEAD_DIM, axis=1)
    return dmask, lanes(xi), lanes(zeta), jnp.repeat(d_chunk, HEAD_DIM)[None, :]


def _rope_tables(pos, nh):
    half = HEAD_DIM // 2
    inv = jnp.power(ROPE_BASE, -jnp.arange(half, dtype=F32) / half)
    ang = pos.astype(F32)[:, None] * inv[None, :]
    cos = jnp.cos(ang)
    sin = jnp.sin(ang)
    return (jnp.tile(jnp.concatenate([cos, cos], axis=1), (1, nh)),
            jnp.tile(jnp.concatenate([-sin, sin], axis=1), (1, nh)))


def _group_mean_matrix(width):
    g = jnp.arange(width) // HEAD_DIM
    return jnp.where(g[:, None] == g[None, :], 1.0 / HEAD_DIM, 0.0).astype(BF16)


def _suffix_matrix(n):
    i = jnp.arange(n)
    return (i[:, None] >= i[None, :]).astype(BF16)


def _layer_weights(i, wts):
    (norm_mix_gain, w_in, lam_re, lam_im, log_dt, b_re, b_im, c_re, c_im, s5_d, glu_w, glu_b,
     ret_gn, q_gain, k_gain, sb_bias, merge, w_out, norm_ffn, w_up, w_gate, conv_w, conv_b, w_down,
     ple_norm, ple_gate, ple_proj) = [t[i] for t in wts]
    nh_s = sb_bias.shape[0]
    row = lambda v: v.reshape(1, -1)
    are, aim, bblk, cblk = _s5_params(lam_re, lam_im, log_dt, b_re, b_im, c_re, c_im)
    conv_w8 = jnp.concatenate([conv_w, jnp.zeros((SUBLANES - CONV_W, conv_w.shape[1]), F32)], axis=0)
    return dict(
        norm_mix=row(norm_mix_gain), w_in=w_in.astype(BF16),
        are=are, aim=aim, bblk=bblk, cblk=cblk, d_skip=row(s5_d), glu_w=glu_w.astype(BF16),
        glu_b=row(glu_b), ret_gn=row(ret_gn),
        q_gain=row(jnp.tile(q_gain, nh_s)), k_gain=row(jnp.tile(k_gain, nh_s)), sb_bias=sb_bias,
        merge=row(merge), w_out=w_out.astype(BF16), norm_ffn=row(norm_ffn),
        w_up=w_up.astype(BF16), w_gate=w_gate.astype(BF16), conv_w=conv_w8, conv_b=row(conv_b),
        w_down=w_down.astype(BF16),
        ple_norm=row(ple_norm), ple_gate=ple_gate.astype(BF16), ple_proj=ple_proj.astype(BF16))


def _prompt_layer(h2, p2, w, consts, kv_prev, *, layer, nb, seq):
    wa, wr, ws, nh_r = consts["wa"], consts["wr"], consts["ws"], consts["nh_r"]
    tm, tq, chunk, tc = consts["tm"], consts["tq"], consts["chunk"], consts["tc"]
    ua, qr, kr, vr, gr, qs, vs, kst, vst = _proj_in(
        h2, w["norm_mix"], w["w_in"], consts["cos"], consts["sin"], w["q_gain"], w["k_gain"],
        consts["gm_s"], kv_prev, nb=nb, seq=seq, tm=tm, wa=wa, wr=wr, ws=ws, n_prev=layer)
    zeros = jnp.zeros((nb, w["are"].shape[1]), F32)
    ya, sre, sim = _s5(ua.reshape(seq * nb, wa), zeros, zeros, w["are"], w["aim"], w["bblk"],
                       w["cblk"], w["d_skip"], w["glu_w"], w["glu_b"], nb=nb, seq=seq, tc=tc)
    seq3 = lambda a: a.reshape(nb, seq, wr)
    yr, sret = _retention(seq3(qr), seq3(kr), seq3(vr), seq3(gr), consts["dmask"], consts["xi"],
                          consts["zeta"], consts["dcol"], w["ret_gn"], consts["gm_r"], nb=nb,
                          seq=seq, chunk=chunk, nh=nh_r, nseq=consts["ret_nseq"])
    yr = yr.reshape(nb * seq, wr)
    ys = _stick_breaking(qs, kst, vs, w["sb_bias"], consts["later_q"], layer=layer, nb=nb,
                         seq=seq, tq=tq)
    h_out, conv = _mix_ffn(h2, ya.reshape(seq, nb * wa), yr, ys, p2, w, nb=nb, seq=seq, tm=tm)
    ret_state = jnp.stack([sret[:, h * HEAD_DIM:(h + 1) * HEAD_DIM, h * HEAD_DIM:(h + 1) * HEAD_DIM]
                           for h in range(nh_r)], axis=1)
    conv_state = conv[:, SUBLANES - (CONV_W - 1):, :]
    return h_out, (kst, vst), (sre, sim, ret_state, conv_state)


def _decode_layer(h2, p2, w, consts, past, *, layer, nb, ntok):
    wa, wr, ws, nh_r, nh_s = consts["wa"], consts["wr"], consts["ws"], consts["nh_r"], consts["nh_s"]
    rows = nb * ntok
    ua, qr, kr, vr, gr, qs, ks, vs = _proj_in(
        h2, w["norm_mix"], w["w_in"], consts["cos"], consts["sin"], w["q_gain"], w["k_gain"],
        consts["gm_s"], nb=1, seq=rows, tm=rows, wa=wa, wr=wr, ws=ws)
    to_tm = lambda a: a.reshape(nb, ntok, -1).transpose(1, 0, 2).reshape(rows, -1)
    from_tm = lambda a: a.reshape(ntok, nb, -1).transpose(1, 0, 2).reshape(rows, -1)
    ya, sre, sim = _s5(to_tm(ua), past["ssm_re"], past["ssm_im"], w["are"], w["aim"], w["bblk"],
                       w["cblk"], w["d_skip"], w["glu_w"], w["glu_b"], nb=nb, seq=ntok, tc=ntok)
    s0cat = past["ret"].transpose(0, 2, 1, 3).reshape(nb * HEAD_DIM, wr)
    yr, snew = _retention_step(qr, kr, vr, gr, consts["dmask"], consts["xi"], consts["zeta"],
                               consts["dcol"], w["ret_gn"], consts["gm_r"], s0cat,
                               consts["expand"], nh=nh_r, ntok=ntok)
    ret_state = snew.reshape(nb, HEAD_DIM, nh_r, HEAD_DIM).transpose(0, 2, 1, 3)
    q_rep = jnp.repeat(qs.reshape(nb, ntok, ws), nh_s, axis=1)
    q_exp = jnp.where(consts["q_own"], q_rep, 0.0)
    pad = lambda a: jnp.pad(a.reshape(nb, ntok, ws), ((0, 0), (0, -ntok % SUBLANES), (0, 0)))
    bias_rows = jnp.tile(w["sb_bias"], ntok).reshape(ntok * nh_s, 1)
    ys = _stick_breaking_step(past["page_table"], q_exp, pad(ks), pad(vs), bias_rows,
                              consts["later_p"], past["cache_k"], past["cache_v"], layer=layer,
                              pp=consts["pp"], nh=nh_s).reshape(rows, ws)
    buf = past["conv"]
    zero = jnp.zeros_like(buf[:, :1])
    prev1 = jnp.concatenate([buf[:, 1:2]] + [zero] * (ntok - 1), axis=1).reshape(rows, -1)
    prev2 = jnp.concatenate([buf[:, 0:1], buf[:, 1:2]] + [zero] * (ntok - 2), axis=1).reshape(rows, -1)
    h_out, conv = _mix_ffn(h2, from_tm(ya), yr, ys, p2, w, (prev1, prev2),
                           nb=1, seq=rows, tm=rows, decode_tokens=ntok)
    conv_state = conv.reshape(nb, ntok, -1)[:, ntok - (CONV_W - 1):, :]
    return h_out, (ks, vs, sre, sim, ret_state, conv_state)


def kernel(x_prompt, x_sample, cache_k, cache_v, state_ssm_re, state_ssm_im, state_ret, state_conv, page_table, p_prompt, p_sample, norm_mix_gain, w_in, s5_lambda_re, s5_lambda_im, s5_log_dt, s5_b_re, s5_b_im, s5_c_re, s5_c_im, s5_d, s5_glu_w, s5_glu_b, ret_gn_gain, sb_q_gain, sb_k_gain, sb_logit_bias, merge_gain, w_out, norm_ffn_gain, ffn_w_up, ffn_w_gate, ffn_conv_w, ffn_conv_b, ffn_w_down, ple_norm_gain, ple_w_gate, ple_w_proj):
    wts = (norm_mix_gain, w_in, s5_lambda_re, s5_lambda_im, s5_log_dt, s5_b_re, s5_b_im, s5_c_re,
           s5_c_im, s5_d, s5_glu_w, s5_glu_b, ret_gn_gain, sb_q_gain, sb_k_gain, sb_logit_bias,
           merge_gain, w_out, norm_ffn_gain, ffn_w_up, ffn_w_gate, ffn_conv_w, ffn_conv_b,
           ffn_w_down, ple_norm_gain, ple_w_gate, ple_w_proj)
    depth = w_in.shape[0]
    nb, seq, d = x_prompt.shape
    nbs, ntok, _ = x_sample.shape
    past_len = page_table.shape[1] * PAGE_SIZE
    wa = s5_d.shape[1]
    wr = ret_gn_gain.shape[1]
    nh_r = wr // HEAD_DIM
    nh_s = sb_logit_bias.shape[1]
    ws = nh_s * HEAD_DIM
    n_pool = cache_k.shape[1]
    rows_s = nbs * ntok

    shared = dict(wa=wa, wr=wr, ws=ws, nh_r=nh_r, nh_s=nh_s, gm_s=_group_mean_matrix(ws),
                  gm_r=_group_mean_matrix(wr))
    chunk = min(128, seq)
    cos_p, sin_p = _rope_tables(jnp.arange(seq, dtype=jnp.int32), nh_r)
    dmask_p, xi_p, zeta_p, dcol_p = _ret_consts(nh_r, chunk, chunk)
    tq = min(256, seq)
    consts_p = dict(shared, cos=cos_p, sin=sin_p, dmask=dmask_p, xi=xi_p, zeta=zeta_p, dcol=dcol_p,
                    tm=min(512, seq), tq=tq, chunk=chunk, tc=min(128, seq),
                    later_q=_suffix_matrix(tq), ret_nseq=RET_SEQS if nb % RET_SEQS == 0 else 1)
    pos_s = past_len + jnp.tile(jnp.arange(ntok, dtype=jnp.int32), nbs)
    cos_s, sin_s = _rope_tables(pos_s, nh_r)
    dmask_s, xi_s, zeta_s, dcol_s = _ret_consts(nh_r, rows_s, ntok)
    lane_src = jnp.arange(wr)
    lane_dst = jnp.arange(nbs * HEAD_DIM)
    expand = jnp.stack([(lane_src[:, None] == h * HEAD_DIM + lane_dst[None, :] % HEAD_DIM)
                        for h in range(nh_r)]).astype(BF16)
    q_own = (jnp.arange(ws)[None, :] // HEAD_DIM) == (jnp.arange(ntok * nh_s)[:, None] % nh_s)
    consts_s = dict(shared, cos=cos_s, sin=sin_s, dmask=dmask_s, xi=xi_s, zeta=zeta_s, dcol=dcol_s,
                    expand=expand, q_own=q_own[None], later_p=_suffix_matrix(PAGE_SIZE),
                    pp=min(PAGES_PER_STEP, page_table.shape[1]))

    ck = cache_k.transpose(0, 1, 3, 4, 2).reshape(depth, n_pool, ws, PAGE_SIZE)
    cv = cache_v.transpose(0, 1, 3, 4, 2).reshape(depth, n_pool, ws, PAGE_SIZE)
    h_p = x_prompt.reshape(nb * seq, d)
    h_s = x_sample.reshape(rows_s, d)
    outs_p, outs_s = [], []
    kv_p = None
    for i in range(depth):
        w = _layer_weights(i, wts)
        h_p, kv_p, st_p = _prompt_layer(h_p, p_prompt[i].reshape(nb * seq, -1), w, consts_p, kv_p,
                                        layer=i, nb=nb, seq=seq)
        past = dict(page_table=page_table, cache_k=ck, cache_v=cv,
                    ssm_re=state_ssm_re[i].reshape(nbs, -1), ssm_im=state_ssm_im[i].reshape(nbs, -1),
                    ret=state_ret[i], conv=state_conv[i])
        h_s, st_s = _decode_layer(h_s, p_sample[i].reshape(rows_s, -1), w, consts_s, past,
                                  layer=i, nb=nbs, ntok=ntok)
        outs_p.append(st_p)
        outs_s.append(st_s)

    st_shape = lambda b: (depth, b, wa // S5_CH, S5_STATE)
    stack = lambda outs: [jnp.stack(x, axis=0) for x in zip(*outs)]
    k_p, v_p = [t.reshape(depth, nb, nh_s, HEAD_DIM, seq).transpose(0, 1, 4, 2, 3) for t in kv_p]
    sre_p, sim_p, ret_p, conv_p = stack(outs_p)
    k_s, v_s, sre_s, sim_s, ret_s, conv_s = stack(outs_s)
    kv_shape_s = (depth, nbs, ntok, nh_s, HEAD_DIM)
    return (h_p.reshape(nb, seq, d), h_s.reshape(nbs, ntok, d),
            k_p, v_p, sre_p.reshape(st_shape(nb)), sim_p.reshape(st_shape(nb)), ret_p, conv_p,
            k_s.reshape(kv_shape_s), v_s.reshape(kv_shape_s), sre_s.reshape(st_shape(nbs)),
            sim_s.reshape(st_shape(nbs)), ret_s, conv_s)
```

```python
import functools

import jax
import jax.numpy as jnp
from jax import lax
from jax.experimental import pallas as pl
from jax.experimental.pallas import tpu as pltpu

F32 = jnp.float32
BF16 = jnp.bfloat16

EPS = 1e-6
LOG2E = 1.4426950408889634
ROPE_BASE = 10000.0
HEAD_DIM = 64
S5_CH = 16
S5_STATE = 64
PAGE_SIZE = 128
CONV_W = 3
LANES = 128
SUBLANES = 8
FF_CHUNK = 256
DOWN_GROUP = 4
RET_SEQS = 4
BIAS_PIECES = 3
PAGES_PER_STEP = 16
VMEM_LIMIT = 56 * 1024 * 1024

NT_DIMS = (((1,), (1,)), ((), ()))


def _dot(a, b):
    return jnp.dot(a.astype(BF16), b.astype(BF16), preferred_element_type=F32)


def _dot_nt(a, b):
    return lax.dot_general(a.astype(BF16), b.astype(BF16), NT_DIMS, preferred_element_type=F32)


def _dot_split(x, m):
    hi = x.astype(BF16)
    lo = (x - hi.astype(F32)).astype(BF16)
    return (jnp.dot(hi, m, preferred_element_type=F32)
            + jnp.dot(lo, m, preferred_element_type=F32))


def _rms(x, gain):
    return x * lax.rsqrt(jnp.mean(x * x, axis=-1, keepdims=True) + EPS) * gain


def _params(sem):
    return pltpu.CompilerParams(dimension_semantics=sem, vmem_limit_bytes=VMEM_LIMIT)


def _const_spec(shape, single_buffer=True):
    n = len(shape)
    if single_buffer:
        return pl.BlockSpec(shape, lambda *_: (0,) * n, pipeline_mode=pl.Buffered(1))
    return pl.BlockSpec(shape, lambda *_: (0,) * n)


def _proj_in_kernel(*refs, wa, wr, ws, n_prev):
    h_ref, g_ref, w_ref, cos_ref, sin_ref, qg_ref, kg_ref, gm_ref = refs[:8]
    refs = refs[8:]
    if n_prev > 0:
        kprev_ref, vprev_ref = refs[:2]
        refs = refs[2:]
    if n_prev < 0:
        ua_ref, qr_ref, kr_ref, vr_ref, gr_ref, qs_ref, ks_ref, vs_ref = refs
    else:
        ua_ref, qr_ref, kr_ref, vr_ref, gr_ref, qs_ref, vs_ref, kst_ref, vst_ref = refs
    n = _rms(h_ref[...], g_ref[...]).astype(BF16)

    def seg(lo, width):
        return jnp.dot(n, w_ref[:, lo:lo + width], preferred_element_type=F32)

    cos = cos_ref[...]
    sin = sin_ref[...]
    lane = lax.broadcasted_iota(jnp.int32, (1, wr), 1)
    first_half = (lane & (HEAD_DIM - 1)) < HEAD_DIM // 2

    def rope(v):
        swapped = jnp.where(first_half, pltpu.roll(v, wr - HEAD_DIM // 2, 1),
                            pltpu.roll(v, HEAD_DIM // 2, 1))
        return v * cos + swapped * sin

    gm = gm_ref[...]

    def head_rms(v, gain):
        ms = jnp.dot((v * v).astype(BF16), gm, preferred_element_type=F32)
        return v * lax.rsqrt(ms + EPS) * gain

    o = 0
    ua_ref[...] = seg(o, wa)
    o += wa
    qr_ref[...] = rope(seg(o, wr))
    o += wr
    kr_ref[...] = rope(seg(o, wr)) * (HEAD_DIM ** -0.5)
    o += wr
    vr_ref[...] = seg(o, wr)
    o += wr
    gr_ref[...] = seg(o, wr)
    o += wr
    qs_ref[...] = head_rms(seg(o, ws), qg_ref[...]) * (LOG2E * HEAD_DIM ** -0.5)
    o += ws
    ks = head_rms(seg(o, ws), kg_ref[...])
    o += ws
    vs = seg(o, ws)
    vs_ref[...] = vs
    if n_prev < 0:
        ks_ref[...] = ks
    else:
        if n_prev > 0:
            kst_ref[0:n_prev] = kprev_ref[...]
            vst_ref[0:n_prev] = vprev_ref[...]
        kst_ref[n_prev] = ks.T
        vst_ref[n_prev] = vs.T


def _proj_in(h2, gain, w_in, cos_t, sin_t, q_gain, k_gain, gm, kv_prev=None, *, nb, seq, tm, wa, wr,
             ws, n_prev=-1):
    d = h2.shape[1]
    nt = seq // tm
    row = lambda b, t: (b * nt + t, 0)
    tspec = lambda w: pl.BlockSpec((tm, w), row)
    args = [h2, gain, w_in, cos_t, sin_t, q_gain, k_gain, gm]
    in_specs = [tspec(d), _const_spec((1, d)), _const_spec(w_in.shape),
                pl.BlockSpec((tm, wr), lambda b, t: (t, 0)),
                pl.BlockSpec((tm, wr), lambda b, t: (t, 0)),
                _const_spec((1, ws)), _const_spec((1, ws)), _const_spec((ws, ws))]
    out_shape = [jax.ShapeDtypeStruct((seq, nb * wa), F32)]
    out_shape += [jax.ShapeDtypeStruct((nb * seq, wr), F32)] * 4
    out_shape += [jax.ShapeDtypeStruct((nb * seq, ws), F32)] * 2
    out_specs = [pl.BlockSpec((tm, wa), lambda b, t: (t, b))] + [tspec(wr)] * 4 + [tspec(ws)] * 2
    if n_prev < 0:
        out_shape += [jax.ShapeDtypeStruct((nb * seq, ws), F32)]
        out_specs += [tspec(ws)]
    else:
        stack = lambda n: pl.BlockSpec((n, ws, tm), lambda b, t: (0, b, t))
        if n_prev > 0:
            args += list(kv_prev)
            in_specs += [stack(n_prev)] * 2
        out_shape += [jax.ShapeDtypeStruct((n_prev + 1, nb * ws, seq), F32)] * 2
        out_specs += [stack(n_prev + 1)] * 2
    return pl.pallas_call(
        functools.partial(_proj_in_kernel, wa=wa, wr=wr, ws=ws, n_prev=n_prev),
        grid=(nb, nt),
        in_specs=in_specs,
        out_specs=out_specs,
        out_shape=out_shape,
        compiler_params=_params(("arbitrary", "arbitrary")),
        name="proj_in",
    )(*args)


def _s5_kernel(u_ref, s0re_ref, s0im_ref, are_ref, aim_ref, bblk_ref, cblk_ref, dsk_ref,
               gw_ref, gb_ref, y_ref, fre_ref, fim_ref, drv_scr, hre_scr, him_scr, *, nb, tc):
    ns = hre_scr.shape[1]

    @pl.when(pl.program_id(0) == 0)
    def _():
        hre_scr[...] = s0re_ref[...]
        him_scr[...] = s0im_ref[...]

    u = u_ref[...]
    drv_scr[...] = _dot(u, bblk_ref[...])
    are = jnp.broadcast_to(are_ref[...], (nb, ns))
    aim = jnp.broadcast_to(aim_ref[...], (nb, ns))

    def step(t, carry):
        hre, him = carry
        rows = pl.ds(pl.multiple_of(t * nb, nb), nb)
        nre = are * hre - aim * him + drv_scr[rows, 0:ns]
        nim = are * him + aim * hre + drv_scr[rows, ns:2 * ns]
        drv_scr[rows, 0:ns] = nre
        drv_scr[rows, ns:2 * ns] = nim
        return nre, nim

    hre, him = lax.fori_loop(0, tc, step, (hre_scr[...], him_scr[...]))
    hre_scr[...] = hre
    him_scr[...] = him
    fre_ref[...] = hre
    fim_ref[...] = him
    y = _dot(drv_scr[...], cblk_ref[...]) + dsk_ref[...] * u
    gy = jax.nn.gelu(y)
    y_ref[...] = gy * jax.nn.sigmoid(_dot(gy, gw_ref[...]) + gb_ref[...])


def _s5(u_tm, s0re, s0im, are, aim, bblk, cblk, dsk, gw, gb, *, nb, seq, tc):
    wa = u_tm.shape[1]
    ns = are.shape[1]
    rows = tc * nb
    return pl.pallas_call(
        functools.partial(_s5_kernel, nb=nb, tc=tc),
        grid=(seq // tc,),
        in_specs=[pl.BlockSpec((rows, wa), lambda i: (i, 0)),
                  _const_spec((nb, ns)), _const_spec((nb, ns)),
                  _const_spec((1, ns)), _const_spec((1, ns)),
                  _const_spec(bblk.shape), _const_spec(cblk.shape), _const_spec((1, wa)),
                  _const_spec(gw.shape), _const_spec((1, wa))],
        out_specs=[pl.BlockSpec((rows, wa), lambda i: (i, 0)),
                   pl.BlockSpec((nb, ns), lambda i: (0, 0)),
                   pl.BlockSpec((nb, ns), lambda i: (0, 0))],
        out_shape=[jax.ShapeDtypeStruct((seq * nb, wa), F32),
                   jax.ShapeDtypeStruct((nb, ns), F32),
                   jax.ShapeDtypeStruct((nb, ns), F32)],
        scratch_shapes=[pltpu.VMEM((rows, 2 * ns), F32), pltpu.VMEM((nb, ns), F32),
                        pltpu.VMEM((nb, ns), F32)],
        compiler_params=_params(("arbitrary",)),
        name="s5",
    )(u_tm, s0re, s0im, are, aim, bblk, cblk, dsk, gw, gb)


def _head_lane_mask(width, h):
    lane = lax.broadcasted_iota(jnp.int32, (1, width), 1)
    return (lane >> 6) == h


def _ret_intra(q, k, v, dmask_ref, nh):
    wr = q.shape[1]
    o = jnp.zeros(q.shape, F32)
    for h in range(nh):
        mh = _head_lane_mask(wr, h)
        s = _dot_nt(jnp.where(mh, q, 0.0), k) * dmask_ref[h]
        o = jnp.where(mh, _dot(s, v), o)
    return o


def _ret_finish(o, gate, gn_ref, gm_ref):
    gm = gm_ref[...]
    xc = o - _dot_split(o, gm)
    var = _dot_split(xc * xc, gm)
    return jax.nn.silu(gate) * (xc * lax.rsqrt(var + EPS) * gn_ref[...])


def _ret_kernel(q_ref, k_ref, v_ref, g_ref, dmask_ref, xi_ref, zeta_ref, dcol_ref, gn_ref, gm_ref,
                y_ref, sfin_ref, s_scr, *, nh):
    nseq, _, wr = q_ref.shape

    @pl.when(pl.program_id(1) == 0)
    def _():
        s_scr[...] = jnp.zeros(s_scr.shape, F32)

    row_h = lax.broadcasted_iota(jnp.int32, (wr, wr), 0) >> 6
    col_h = lax.broadcasted_iota(jnp.int32, (wr, wr), 1) >> 6
    for i in range(nseq):
        q = q_ref[i]
        k = k_ref[i]
        v = v_ref[i]
        s_prev = s_scr[i]
        o = _ret_intra(q, k, v, dmask_ref, nh) + _dot(q * xi_ref[...], s_prev)
        kv = _dot((k * zeta_ref[...]).T, v)
        s_new = s_prev * dcol_ref[...] + jnp.where(row_h == col_h, kv, 0.0)
        s_scr[i] = s_new
        sfin_ref[i] = s_new
        y_ref[i] = _ret_finish(o, g_ref[i], gn_ref, gm_ref)


def _retention(q, k, v, g, dmask, xi, zeta, dcol, gn, gm, *, nb, seq, chunk, nh, nseq):
    wr = q.shape[-1]
    nc = seq // chunk
    tspec = pl.BlockSpec((nseq, chunk, wr), lambda b, c: (b, c, 0))
    return pl.pallas_call(
        functools.partial(_ret_kernel, nh=nh),
        grid=(nb // nseq, nc),
        in_specs=[tspec] * 4 + [_const_spec(dmask.shape), _const_spec(xi.shape),
                                _const_spec(zeta.shape), _const_spec((1, wr)),
                                _const_spec((1, wr)), _const_spec((wr, wr))],
        out_specs=[tspec, pl.BlockSpec((nseq, wr, wr), lambda b, c: (b, 0, 0))],
        out_shape=[jax.ShapeDtypeStruct((nb, seq, wr), F32),
                   jax.ShapeDtypeStruct((nb, wr, wr), F32)],
        scratch_shapes=[pltpu.VMEM((nseq, wr, wr), F32)],
        compiler_params=_params(("arbitrary", "arbitrary")),
        name="retention",
    )(q, k, v, g, dmask, xi, zeta, dcol, gn, gm)


def _ret_step_kernel(q_ref, k_ref, v_ref, g_ref, dmask_ref, xi_ref, zeta_ref, dcol_ref, gn_ref,
                     gm_ref, s0_ref, exp_ref, y_ref, snew_ref, *, nh, ntok):
    t_rows, wr = q_ref.shape
    sw = s0_ref.shape[0]
    q = q_ref[...]
    k = k_ref[...]
    v = v_ref[...]
    s0 = s0_ref[...]
    o = _ret_intra(q, k, v, dmask_ref, nh)
    qx = q * xi_ref[...]
    kz = k * zeta_ref[...]
    own_b = ((lax.broadcasted_iota(jnp.int32, (t_rows, sw), 1) >> 6)
             == lax.broadcasted_iota(jnp.int32, (t_rows, sw), 0) // ntok)
    s_new = s0 * dcol_ref[...]
    for h in range(nh):
        mh = _head_lane_mask(wr, h)
        q_exp = jnp.where(own_b, _dot(jnp.where(mh, qx, 0.0), exp_ref[h]), 0.0)
        k_exp = jnp.where(own_b, _dot(jnp.where(mh, kz, 0.0), exp_ref[h]), 0.0)
        o = o + jnp.where(mh, _dot(q_exp, s0), 0.0)
        s_new = s_new + jnp.where(mh, _dot(k_exp.T, v), 0.0)
    snew_ref[...] = s_new
    y_ref[...] = _ret_finish(o, g_ref[...], gn_ref, gm_ref)


def _retention_step(q, k, v, g, dmask, xi, zeta, dcol, gn, gm, s0cat, expand, *, nh, ntok):
    t_rows, wr = q.shape
    args = (q, k, v, g, dmask, xi, zeta, dcol, gn, gm, s0cat, expand)
    return pl.pallas_call(
        functools.partial(_ret_step_kernel, nh=nh, ntok=ntok),
        grid=(1,),
        in_specs=[_const_spec(a.shape) for a in args],
        out_specs=[_const_spec((t_rows, wr), False), _const_spec(s0cat.shape, False)],
        out_shape=[jax.ShapeDtypeStruct((t_rows, wr), F32),
                   jax.ShapeDtypeStruct(s0cat.shape, F32)],
        compiler_params=_params(("arbitrary",)),
        name="retention_step",
    )(*args)


def _sb_fail(z2, causal):
    g = jnp.maximum(z2, 0.0) + jnp.log(1.0 + jnp.exp2(-jnp.abs(z2))) * LOG2E
    return g if causal is None else jnp.where(causal, g, 0.0)


def _sb_weight(z2, suffix, run, causal):
    w = jnp.exp2(z2 - suffix - run)
    return w if causal is None else jnp.where(causal, w, 0.0)


def _sb_kernel(ext_ref, q_ref, k_ref, v_ref, u_ref, o_ref, acc_scr, run_scr, *, tq):
    qi = pl.program_id(2)
    q = q_ref[...]
    width = q.shape[1]
    nh = width // HEAD_DIM
    rows = nh * tq
    umat = u_ref[...]
    q2 = jnp.concatenate([jnp.where(_head_lane_mask(width, h), q, 0.0) for h in range(nh)],
                         axis=0).astype(BF16)
    q2 = jnp.concatenate([q2, ext_ref[...]], axis=1)
    ones = (lax.broadcasted_iota(jnp.int32, (LANES, tq), 0) < BIAS_PIECES).astype(BF16)
    causal = (lax.broadcasted_iota(jnp.int32, (rows, tq), 1)
              < (lax.broadcasted_iota(jnp.int32, (rows, tq), 0) & (tq - 1)))

    def tiles(j, count, mask):
        acc = acc_scr[...]
        run = run_scr[...]
        keys = [pl.ds(pl.multiple_of((j - i) * tq, tq), tq) for i in range(count)]
        zs = [jnp.dot(q2, jnp.concatenate([k_ref[:, ks].astype(BF16), ones], axis=0),
                      preferred_element_type=F32) for ks in keys]
        suffixes = [_dot(_sb_fail(z, mask), umat) for z in zs]
        for z, suffix, ks in zip(zs, suffixes, keys):
            acc = acc + _dot(_sb_weight(z, suffix, run, mask), v_ref[ks, :])
            run = run + suffix[:, 0:1]
        acc_scr[...] = acc
        run_scr[...] = run

    acc_scr[...] = jnp.zeros(acc_scr.shape, F32)
    run_scr[...] = jnp.zeros(run_scr.shape, F32)
    tiles(qi, 1, causal)
    top = qi - 1
    for bit, count in ((0, 1), (1, 2), (2, 4)):
        has = (qi >> bit) & 1

        @pl.when(has == 1)
        def _(top=top, count=count):
            tiles(top, count, None)

        top = top - has * count

    @pl.loop(0, qi >> 3)
    def _(n):
        tiles(top - 8 * n, 8, None)

    acc = acc_scr[...]
    out = acc[0:tq]
    for h in range(1, nh):
        out = jnp.where(_head_lane_mask(width, h), acc[h * tq:(h + 1) * tq], out)
    o_ref[...] = out


def _stick_breaking(q, kt, v, ext, umat, *, layer, nb, seq, tq):
    ws = q.shape[1]
    nq = seq // tq
    nhp = ws // LANES
    qspec = pl.BlockSpec((tq, LANES), lambda b, hp, i: (b * nq + i, hp))
    kspec = pl.BlockSpec((None, LANES, seq), lambda b, hp, i: (layer, b * nhp + hp, 0))
    return pl.pallas_call(
        functools.partial(_sb_kernel, tq=tq),
        grid=(nb, nhp, nq),
        in_specs=[pl.BlockSpec((None,) + ext.shape[1:], lambda b, hp, i: (hp, 0, 0)), qspec, kspec,
                  pl.BlockSpec((seq, LANES), lambda b, hp, i: (b, hp)), _const_spec((tq, tq))],
        out_specs=qspec,
        out_shape=jax.ShapeDtypeStruct((nb * seq, ws), F32),
        scratch_shapes=[pltpu.VMEM((ext.shape[1], LANES), F32), pltpu.VMEM((ext.shape[1], 1), F32)],
        compiler_params=_params(("arbitrary", "arbitrary", "arbitrary")),
        name="stick_breaking",
    )(ext, q, kt, v, umat)


def _bias_pieces(bias, tq):
    left = bias.astype(F32) * LOG2E
    parts = []
    for _ in range(BIAS_PIECES):
        part = left.astype(BF16)
        parts.append(part)
        left = left - part.astype(F32)
    pieces = jnp.stack(parts, axis=1)
    pieces = jnp.pad(pieces, ((0, 0), (0, LANES - BIAS_PIECES)))
    nh = LANES // HEAD_DIM
    return jnp.repeat(pieces, tq, axis=0).reshape(bias.shape[0] // nh, nh * tq, LANES)


def _sb_step_kernel(pt_ref, q_ref, kn_ref, vn_ref, bias_ref, u_ref, *rest, pp, nh):
    k_pages = rest[:pp]
    v_pages = rest[pp:2 * pp]
    o_ref = rest[2 * pp]
    acc_scr, run_scr = rest[2 * pp + 1:]
    j = pl.program_id(1)
    q = q_ref[...].astype(BF16)
    rows, width = q.shape
    bias = bias_ref[...] * LOG2E
    umat = u_ref[...]

    def sweep(z_pages, causal):
        n = len(z_pages)
        z = jnp.concatenate(z_pages, axis=0) + jnp.concatenate([bias] * n, axis=0)
        suffix = _dot(_sb_fail(z, causal), umat)
        run = run_scr[...]
        later = [None] * n
        for p in reversed(range(n)):
            later[p] = run
            run = run + suffix[p * rows:(p + 1) * rows, 0:1]
        w = _sb_weight(z, suffix, jnp.concatenate(later, axis=0), causal)
        run_scr[...] = run
        return [w[p * rows:(p + 1) * rows] for p in range(n)]

    @pl.when(j == 0)
    def _():
        run_scr[...] = jnp.zeros(run_scr.shape, F32)
        pad = jnp.zeros((PAGE_SIZE - kn_ref.shape[0], width), F32)
        kn = jnp.concatenate([kn_ref[...], pad], axis=0)
        vn = jnp.concatenate([vn_ref[...], pad], axis=0)
        row_t = lax.broadcasted_iota(jnp.int32, (rows, PAGE_SIZE), 0) // nh
        col = lax.broadcasted_iota(jnp.int32, (rows, PAGE_SIZE), 1)
        (w,) = sweep([_dot_nt(q, kn)], col < row_t)
        acc_scr[...] = _dot(w, vn)

    @pl.when(j > 0)
    def _():
        kt = jnp.concatenate([k[...].astype(BF16) for k in k_pages], axis=1)
        z = jnp.dot(q, kt, preferred_element_type=F32)
        w = sweep([z[:, p * PAGE_SIZE:(p + 1) * PAGE_SIZE] for p in range(pp)], None)
        vt = jnp.concatenate([v[...].astype(BF16) for v in v_pages], axis=1)
        acc_scr[...] += _dot_nt(jnp.concatenate(w, axis=1), vt)

    @pl.when(j == pl.num_programs(1) - 1)
    def _():
        lane_h = lax.broadcasted_iota(jnp.int32, (rows, width), 1) >> 6
        row_h = lax.broadcasted_iota(jnp.int32, (rows, width), 0) % nh
        own = jnp.where(lane_h == row_h, acc_scr[...], 0.0)
        o_ref[...] = jnp.sum(own.reshape(rows // nh, nh, width), axis=1)


def _stick_breaking_step(page_table, q_exp, k_new, v_new, bias_rows, umat, cache_k, cache_v, *,
                         layer, pp, nh):
    nb, rows, width = q_exp.shape
    n_pages = page_table.shape[1]
    n_steps = n_pages // pp
    ntok = rows // nh

    def page_map(slot):
        def index(b, j, pt):
            return (layer, pt[b, n_pages - jnp.maximum(j, 1) * pp + slot], 0, 0)
        return index

    page_specs = [pl.BlockSpec((None, None, width, PAGE_SIZE), page_map(s)) for s in range(pp)]
    per_b = lambda r: pl.BlockSpec((None, r, width), lambda b, j, pt: (b, 0, 0))
    new_spec = per_b(k_new.shape[1])
    grid_spec = pltpu.PrefetchScalarGridSpec(
        num_scalar_prefetch=1,
        grid=(nb, n_steps + 1),
        in_specs=[per_b(rows), new_spec, new_spec,
                  pl.BlockSpec((rows, 1), lambda b, j, pt: (0, 0)),
                  pl.BlockSpec((PAGE_SIZE, PAGE_SIZE), lambda b, j, pt: (0, 0))]
        + page_specs + page_specs,
        out_specs=per_b(ntok),
        scratch_shapes=[pltpu.VMEM((rows, width), F32), pltpu.VMEM((rows, 1), F32)],
    )
    return pl.pallas_call(
        functools.partial(_sb_step_kernel, pp=pp, nh=nh),
        grid_spec=grid_spec,
        out_shape=jax.ShapeDtypeStruct((nb, ntok, width), F32),
        compiler_params=_params(("arbitrary", "arbitrary")),
        name="stick_breaking_step",
    )(page_table, q_exp, k_new, v_new, bias_rows, umat, *([cache_k] * pp), *([cache_v] * pp))


def _mix_ffn_kernel(*refs, tm, wa, wr, decode_tokens):
    if decode_tokens:
        (h_ref, ya_ref, yr_ref, ys_ref, p_ref, mg_ref, wout_ref, nf_ref, wup_ref, wgate_ref,
         cw_ref, cb_ref, wdown_ref, pn_ref, pgate_ref, pproj_ref, prev1_ref, prev2_ref,
         out_ref, conv_ref, ext_scr, acc_scr) = refs
    else:
        (h_ref, ya_ref, yr_ref, ys_ref, p_ref, mg_ref, wout_ref, nf_ref, wup_ref, wgate_ref,
         cw_ref, cb_ref, wdown_ref, pn_ref, pgate_ref, pproj_ref,
         out_ref, conv_ref, ext_scr, acc_scr, carry_scr) = refs

        @pl.when(pl.program_id(1) == 0)
        def _():
            carry_scr[...] = jnp.zeros(carry_scr.shape, F32)

    mg = mg_ref[...]
    a1 = wa + wr
    merged = (_dot(_rms(ya_ref[...], mg[:, 0:wa]), wout_ref[0:wa, :])
              + _dot(_rms(yr_ref[...], mg[:, wa:a1]), wout_ref[wa:a1, :])
              + _dot(_rms(ys_ref[...], mg[:, a1:]), wout_ref[a1:, :]))
    h_mid = h_ref[...] + merged
    n2 = _rms(h_mid, nf_ref[...]).astype(BF16)
    acc_scr[...] = jnp.zeros(acc_scr.shape, F32)
    if decode_tokens:
        ext_scr[0:SUBLANES, :] = jnp.zeros((SUBLANES, FF_CHUNK), F32)
        tok = lax.broadcasted_iota(jnp.int32, (tm, FF_CHUNK), 0) % decode_tokens

    acts = []
    nchunk = wup_ref.shape[1] // FF_CHUNK
    ngroup = -(-nchunk // DOWN_GROUP)
    group_ends = {(i + 1) * nchunk // ngroup for i in range(ngroup)}
    for lo in range(0, wup_ref.shape[1], FF_CHUNK):
        cols = slice(lo, lo + FF_CHUNK)
        u = jnp.dot(n2, wup_ref[:, cols], preferred_element_type=F32)
        g = jnp.dot(n2, wgate_ref[:, cols], preferred_element_type=F32)
        if not decode_tokens:
            ext_scr[0:SUBLANES, :] = carry_scr[:, cols]
        ext_scr[SUBLANES:SUBLANES + tm, :] = u
        prev1 = ext_scr[pl.ds(SUBLANES - 1, tm), :]
        prev2 = ext_scr[pl.ds(SUBLANES - 2, tm), :]
        if decode_tokens:
            prev1 = jnp.where(tok >= 1, prev1, prev1_ref[:, cols])
            prev2 = jnp.where(tok >= 2, prev2, prev2_ref[:, cols])
            conv_ref[:, cols] = u
        else:
            last = u[tm - SUBLANES:tm, :]
            carry_scr[:, cols] = last
            conv_ref[:, cols] = last
        conv = cb_ref[:, cols] + cw_ref[0:1, cols] * prev2
        conv = conv + cw_ref[1:2, cols] * prev1
        conv = conv + cw_ref[2:3, cols] * u
        acts.append((jax.nn.gelu(conv) * g).astype(BF16))
        if (lo + FF_CHUNK) // FF_CHUNK in group_ends:
            width = len(acts) * FF_CHUNK
            acc_scr[...] += jnp.dot(jnp.concatenate(acts, axis=1),
                                    wdown_ref[lo + FF_CHUNK - width:lo + FF_CHUNK, :],
                                    preferred_element_type=F32)
            acts = []
    h2 = h_mid + acc_scr[...]
    gate = jax.nn.sigmoid(_dot(_rms(h2, pn_ref[...]), pgate_ref[...]))
    out_ref[...] = h2 + _dot(p_ref[...], pproj_ref[...]) * gate


def _mix_ffn(h2, ya, yr, ys, p2, w, prev=None, *, nb, seq, tm, decode_tokens=0):
    d = h2.shape[1]
    wa, wr, ws = w["d_skip"].shape[1], yr.shape[1], ys.shape[1]
    pd = p2.shape[1]
    nt = seq // tm
    dff = w["w_up"].shape[1]
    row = lambda b, t: (b * nt + t, 0)
    tspec = lambda width: pl.BlockSpec((tm, width), row)
    ya_spec = tspec(wa) if decode_tokens else pl.BlockSpec((tm, wa), lambda b, t: (t, b))
    weights = [w["merge"], w["w_out"], w["norm_ffn"], w["w_up"], w["w_gate"], w["conv_w"],
               w["conv_b"], w["w_down"], w["ple_norm"], w["ple_gate"], w["ple_proj"]]
    in_specs = [tspec(d), ya_spec, tspec(wr), tspec(ws), tspec(pd)]
    in_specs += [_const_spec(a.shape) for a in weights]
    args = [h2, ya, yr, ys, p2] + weights
    scratch = [pltpu.VMEM((tm + SUBLANES, FF_CHUNK), F32), pltpu.VMEM((tm, d), F32)]
    if decode_tokens:
        in_specs += [_const_spec(prev[0].shape), _const_spec(prev[1].shape)]
        args += list(prev)
        conv_rows = tm
    else:
        scratch.append(pltpu.VMEM((SUBLANES, dff), F32))
        conv_rows = SUBLANES
    return pl.pallas_call(
        functools.partial(_mix_ffn_kernel, tm=tm, wa=wa, wr=wr, decode_tokens=decode_tokens),
        grid=(nb, nt),
        in_specs=in_specs,
        out_specs=[tspec(d),
                   pl.BlockSpec((None, conv_rows, dff), lambda b, t: (b, 0, 0))],
        out_shape=[jax.ShapeDtypeStruct((nb * seq, d), F32),
                   jax.ShapeDtypeStruct((nb, conv_rows, dff), F32)],
        scratch_shapes=scratch,
        compiler_params=_params(("arbitrary", "arbitrary")),
        name="mix_ffn_step" if decode_tokens else "mix_ffn",
    )(*args)


def _block_diag(blocks):
    g, r, c = blocks.shape
    eye = jnp.eye(g, dtype=blocks.dtype)
    return (eye[:, None, :, None] * blocks[:, :, None, :]).reshape(g * r, g * c)


def _s5_params(lam_re, lam_im, log_dt, b_re, b_im, c_re, c_im):
    dt = jnp.exp(log_dt.astype(F32))[:, None]
    lr = lam_re.astype(F32)
    li = lam_im.astype(F32)
    mag = jnp.exp(lr * dt)
    ang = li * dt
    ab_re = mag * jnp.cos(ang)
    ab_im = mag * jnp.sin(ang)
    den = lr * lr + li * li
    f_re = ((ab_re - 1.0) * lr + ab_im * li) / den
    f_im = (ab_im * lr - (ab_re - 1.0) * li) / den
    bb_re = f_re[..., None] * b_re - f_im[..., None] * b_im
    bb_im = f_re[..., None] * b_im + f_im[..., None] * b_re
    to_in = lambda m: _block_diag(jnp.swapaxes(m, 1, 2))
    bblk = jnp.concatenate([to_in(bb_re), to_in(bb_im)], axis=1)
    to_out = lambda m: _block_diag(jnp.swapaxes(m, 1, 2))
    cblk = jnp.concatenate([to_out(c_re), -to_out(c_im)], axis=0)
    return ab_re.reshape(1, -1), ab_im.reshape(1, -1), bblk.astype(BF16), cblk.astype(BF16)


def _ret_consts(nh, chunk, group):
    log_g = jnp.log1p(-jnp.exp2(-5.0 - jnp.arange(nh, dtype=F32)))
    idx = jnp.arange(chunk, dtype=jnp.int32)
    pos = (idx % group).astype(F32)
    diff = pos[:, None] - pos[None, :]
    ok = (diff >= 0) & ((idx // group)[:, None] == (idx // group)[None, :])
    dmask = jnp.where(ok[None], jnp.exp(jnp.where(ok, diff, 0.0)[None] * log_g[:, None, None]), 0.0)
    xi = jnp.exp((pos + 1.0)[None, :] * log_g[:, None])
    zeta = jnp.exp((group - 1.0 - pos)[None, :] * log_g[:, None])
    d_chunk = jnp.exp(group * log_g)
    lanes = lambda t: jnp.repeat(t.T, HEAD_DIM, axis=1)
    return dmask, lanes(xi), lanes(zeta), jnp.repeat(d_chunk, HEAD_DIM)[None, :]


def _rope_tables(pos, nh):
    half = HEAD_DIM // 2
    inv = jnp.power(ROPE_BASE, -jnp.arange(half, dtype=F32) / half)
    ang = pos.astype(F32)[:, None] * inv[None, :]
    cos = jnp.cos(ang)
    sin = jnp.sin(ang)
    return (jnp.tile(jnp.concatenate([cos, cos], axis=1), (1, nh)),
            jnp.tile(jnp.concatenate([-sin, sin], axis=1), (1, nh)))


def _group_mean_matrix(width):
    g = jnp.arange(width) // HEAD_DIM
    return jnp.where(g[:, None] == g[None, :], 1.0 / HEAD_DIM, 0.0).astype(BF16)


def _suffix_matrix(n):
    i = jnp.arange(n)
    return (i[:, None] >= i[None, :]).astype(BF16)


def _layer_weights(i, wts):
    (norm_mix_gain, w_in, lam_re, lam_im, log_dt, b_re, b_im, c_re, c_im, s5_d, glu_w, glu_b,
     ret_gn, q_gain, k_gain, sb_bias, merge, w_out, norm_ffn, w_up, w_gate, conv_w, conv_b, w_down,
     ple_norm, ple_gate, ple_proj) = [t[i] for t in wts]
    nh_s = sb_bias.shape[0]
    row = lambda v: v.reshape(1, -1)
    are, aim, bblk, cblk = _s5_params(lam_re, lam_im, log_dt, b_re, b_im, c_re, c_im)
    conv_w8 = jnp.concatenate([conv_w, jnp.zeros((SUBLANES - CONV_W, conv_w.shape[1]), F32)], axis=0)
    return dict(
        norm_mix=row(norm_mix_gain), w_in=w_in.astype(BF16),
        are=are, aim=aim, bblk=bblk, cblk=cblk, d_skip=row(s5_d), glu_w=glu_w.astype(BF16),
        glu_b=row(glu_b), ret_gn=row(ret_gn),
        q_gain=row(jnp.tile(q_gain, nh_s)), k_gain=row(jnp.tile(k_gain, nh_s)), sb_bias=sb_bias,
        merge=row(merge), w_out=w_out.astype(BF16), norm_ffn=row(norm_ffn),
        w_up=w_up.astype(BF16), w_gate=w_gate.astype(BF16), conv_w=conv_w8, conv_b=row(conv_b),
        w_down=w_down.astype(BF16),
        ple_norm=row(ple_norm), ple_gate=ple_gate.astype(BF16), ple_proj=ple_proj.astype(BF16))


def _prompt_layer(h2, p2, w, consts, kv_prev, *, layer, nb, seq):
    wa, wr, ws, nh_r = consts["wa"], consts["wr"], consts["ws"], consts["nh_r"]
    tm, tq, chunk, tc = consts["tm"], consts["tq"], consts["chunk"], consts["tc"]
    ua, qr, kr, vr, gr, qs, vs, kst, vst = _proj_in(
        h2, w["norm_mix"], w["w_in"], consts["cos"], consts["sin"], w["q_gain"], w["k_gain"],
        consts["gm_s"], kv_prev, nb=nb, seq=seq, tm=tm, wa=wa, wr=wr, ws=ws, n_prev=layer)
    zeros = jnp.zeros((nb, w["are"].shape[1]), F32)
    ya, sre, sim = _s5(ua.reshape(seq * nb, wa), zeros, zeros, w["are"], w["aim"], w["bblk"],
                       w["cblk"], w["d_skip"], w["glu_w"], w["glu_b"], nb=nb, seq=seq, tc=tc)
    seq3 = lambda a: a.reshape(nb, seq, wr)
    yr, sret = _retention(seq3(qr), seq3(kr), seq3(vr), seq3(gr), consts["dmask"], consts["xi"],
                          consts["zeta"], consts["dcol"], w["ret_gn"], consts["gm_r"], nb=nb,
                          seq=seq, chunk=chunk, nh=nh_r, nseq=consts["ret_nseq"])
    yr = yr.reshape(nb * seq, wr)
    ys = _stick_breaking(qs, kst, vs, _bias_pieces(w["sb_bias"], tq), consts["later_q"],
                         layer=layer, nb=nb, seq=seq, tq=tq)
    h_out, conv = _mix_ffn(h2, ya.reshape(seq, nb * wa), yr, ys, p2, w, nb=nb, seq=seq, tm=tm)
    ret_state = jnp.stack([sret[:, h * HEAD_DIM:(h + 1) * HEAD_DIM, h * HEAD_DIM:(h + 1) * HEAD_DIM]
                           for h in range(nh_r)], axis=1)
    conv_state = conv[:, SUBLANES - (CONV_W - 1):, :]
    return h_out, (kst, vst), (sre, sim, ret_state, conv_state)


def _decode_layer(h2, p2, w, consts, past, *, layer, nb, ntok):
    wa, wr, ws, nh_r, nh_s = consts["wa"], consts["wr"], consts["ws"], consts["nh_r"], consts["nh_s"]
    rows = nb * ntok
    ua, qr, kr, vr, gr, qs, ks, vs = _proj_in(
        h2, w["norm_mix"], w["w_in"], consts["cos"], consts["sin"], w["q_gain"], w["k_gain"],
        consts["gm_s"], nb=1, seq=rows, tm=rows, wa=wa, wr=wr, ws=ws)
    to_tm = lambda a: a.reshape(nb, ntok, -1).transpose(1, 0, 2).reshape(rows, -1)
    from_tm = lambda a: a.reshape(ntok, nb, -1).transpose(1, 0, 2).reshape(rows, -1)
    ya, sre, sim = _s5(to_tm(ua), past["ssm_re"], past["ssm_im"], w["are"], w["aim"], w["bblk"],
                       w["cblk"], w["d_skip"], w["glu_w"], w["glu_b"], nb=nb, seq=ntok, tc=ntok)
    s0cat = past["ret"].transpose(0, 2, 1, 3).reshape(nb * HEAD_DIM, wr)
    yr, snew = _retention_step(qr, kr, vr, gr, consts["dmask"], consts["xi"], consts["zeta"],
                               consts["dcol"], w["ret_gn"], consts["gm_r"], s0cat,
                               consts["expand"], nh=nh_r, ntok=ntok)
    ret_state = snew.reshape(nb, HEAD_DIM, nh_r, HEAD_DIM).transpose(0, 2, 1, 3)
    q_rep = jnp.repeat(qs.reshape(nb, ntok, ws), nh_s, axis=1)
    q_exp = jnp.where(consts["q_own"], q_rep, 0.0)
    pad = lambda a: jnp.pad(a.reshape(nb, ntok, ws), ((0, 0), (0, -ntok % SUBLANES), (0, 0)))
    bias_rows = jnp.tile(w["sb_bias"], ntok).reshape(ntok * nh_s, 1)
    ys = _stick_breaking_step(past["page_table"], q_exp, pad(ks), pad(vs), bias_rows,
                              consts["later_p"], past["cache_k"], past["cache_v"], layer=layer,
                              pp=consts["pp"], nh=nh_s).reshape(rows, ws)
    buf = past["conv"]
    zero = jnp.zeros_like(buf[:, :1])
    prev1 = jnp.concatenate([buf[:, 1:2]] + [zero] * (ntok - 1), axis=1).reshape(rows, -1)
    prev2 = jnp.concatenate([buf[:, 0:1], buf[:, 1:2]] + [zero] * (ntok - 2), axis=1).reshape(rows, -1)
    h_out, conv = _mix_ffn(h2, from_tm(ya), yr, ys, p2, w, (prev1, prev2),
                           nb=1, seq=rows, tm=rows, decode_tokens=ntok)
    conv_state = conv.reshape(nb, ntok, -1)[:, ntok - (CONV_W - 1):, :]
    return h_out, (ks, vs, sre, sim, ret_state, conv_state)


def kernel(x_prompt, x_sample, cache_k, cache_v, state_ssm_re, state_ssm_im, state_ret, state_conv, page_table, p_prompt, p_sample, norm_mix_gain, w_in, s5_lambda_re, s5_lambda_im, s5_log_dt, s5_b_re, s5_b_im, s5_c_re, s5_c_im, s5_d, s5_glu_w, s5_glu_b, ret_gn_gain, sb_q_gain, sb_k_gain, sb_logit_bias, merge_gain, w_out, norm_ffn_gain, ffn_w_up, ffn_w_gate, ffn_conv_w, ffn_conv_b, ffn_w_down, ple_norm_gain, ple_w_gate, ple_w_proj):
    wts = (norm_mix_gain, w_in, s5_lambda_re, s5_lambda_im, s5_log_dt, s5_b_re, s5_b_im, s5_c_re,
           s5_c_im, s5_d, s5_glu_w, s5_glu_b, ret_gn_gain, sb_q_gain, sb_k_gain, sb_logit_bias,
           merge_gain, w_out, norm_ffn_gain, ffn_w_up, ffn_w_gate, ffn_conv_w, ffn_conv_b,
           ffn_w_down, ple_norm_gain, ple_w_gate, ple_w_proj)
    depth = w_in.shape[0]
    nb, seq, d = x_prompt.shape
    nbs, ntok, _ = x_sample.shape
    past_len = page_table.shape[1] * PAGE_SIZE
    wa = s5_d.shape[1]
    wr = ret_gn_gain.shape[1]
    nh_r = wr // HEAD_DIM
    nh_s = sb_logit_bias.shape[1]
    ws = nh_s * HEAD_DIM
    n_pool = cache_k.shape[1]
    rows_s = nbs * ntok

    shared = dict(wa=wa, wr=wr, ws=ws, nh_r=nh_r, nh_s=nh_s, gm_s=_group_mean_matrix(ws),
                  gm_r=_group_mean_matrix(wr))
    chunk = min(128, seq)
    cos_p, sin_p = _rope_tables(jnp.arange(seq, dtype=jnp.int32), nh_r)
    dmask_p, xi_p, zeta_p, dcol_p = _ret_consts(nh_r, chunk, chunk)
    tq = min(256, seq)
    consts_p = dict(shared, cos=cos_p, sin=sin_p, dmask=dmask_p, xi=xi_p, zeta=zeta_p, dcol=dcol_p,
                    tm=min(512, seq), tq=tq, chunk=chunk, tc=min(128, seq),
                    later_q=_suffix_matrix(tq), ret_nseq=RET_SEQS if nb % RET_SEQS == 0 else 1)
    pos_s = past_len + jnp.tile(jnp.arange(ntok, dtype=jnp.int32), nbs)
    cos_s, sin_s = _rope_tables(pos_s, nh_r)
    dmask_s, xi_s, zeta_s, dcol_s = _ret_consts(nh_r, rows_s, ntok)
    lane_src = jnp.arange(wr)
    lane_dst = jnp.arange(nbs * HEAD_DIM)
    expand = jnp.stack([(lane_src[:, None] == h * HEAD_DIM + lane_dst[None, :] % HEAD_DIM)
                        for h in range(nh_r)]).astype(BF16)
    q_own = (jnp.arange(ws)[None, :] // HEAD_DIM) == (jnp.arange(ntok * nh_s)[:, None] % nh_s)
    consts_s = dict(shared, cos=cos_s, sin=sin_s, dmask=dmask_s, xi=xi_s, zeta=zeta_s, dcol=dcol_s,
                    expand=expand, q_own=q_own[None], later_p=_suffix_matrix(PAGE_SIZE),
                    pp=min(PAGES_PER_STEP, page_table.shape[1]))

    ck = cache_k.transpose(0, 1, 3, 4, 2).reshape(depth, n_pool, ws, PAGE_SIZE)
    cv = cache_v.transpose(0, 1, 3, 4, 2).reshape(depth, n_pool, ws, PAGE_SIZE)
    h_p = x_prompt.reshape(nb * seq, d)
    h_s = x_sample.reshape(rows_s, d)
    outs_p, outs_s = [], []
    kv_p = None
    for i in range(depth):
        w = _layer_weights(i, wts)
        h_p, kv_p, st_p = _prompt_layer(h_p, p_prompt[i].reshape(nb * seq, -1), w, consts_p, kv_p,
                                        layer=i, nb=nb, seq=seq)
        past = dict(page_table=page_table, cache_k=ck, cache_v=cv,
                    ssm_re=state_ssm_re[i].reshape(nbs, -1), ssm_im=state_ssm_im[i].reshape(nbs, -1),
                    ret=state_ret[i], conv=state_conv[i])
        h_s, st_s = _decode_layer(h_s, p_sample[i].reshape(rows_s, -1), w, consts_s, past,
                                  layer=i, nb=nbs, ntok=ntok)
        outs_p.append(st_p)
        outs_s.append(st_s)

    st_shape = lambda b: (depth, b, wa // S5_CH, S5_STATE)
    stack = lambda outs: [jnp.stack(x, axis=0) for x in zip(*outs)]
    k_p, v_p = [t.reshape(depth, nb, nh_s, HEAD_DIM, seq).transpose(0, 1, 4, 2, 3) for t in kv_p]
    sre_p, sim_p, ret_p, conv_p = stack(outs_p)
    k_s, v_s, sre_s, sim_s, ret_s, conv_s = stack(outs_s)
    kv_shape_s = (depth, nbs, ntok, nh_s, HEAD_DIM)
    return (h_p.reshape(nb, seq, d), h_s.reshape(nbs, ntok, d),
            k_p, v_p, sre_p.reshape(st_shape(nb)), sim_p.reshape(st_shape(nb)), ret_p, conv_p,
            k_s.reshape(kv_shape_s), v_s.reshape(kv_shape_s), sre_s.reshape(st_shape(nbs)),
            sim_s.reshape(st_shape(nbs)), ret_s, conv_s)
```

```python
import functools

import jax
import jax.numpy as jnp
from jax import lax
from jax.experimental import pallas as pl
from jax.experimental.pallas import tpu as pltpu

F32 = jnp.float32
BF16 = jnp.bfloat16

EPS = 1e-6
LOG2E = 1.4426950408889634
ROPE_BASE = 10000.0
HEAD_DIM = 64
S5_CH = 16
S5_STATE = 64
PAGE_SIZE = 128
CONV_W = 3
LANES = 128
SUBLANES = 8
FF_CHUNK = 256
DOWN_GROUP = 4
RET_SEQS = 4
BIAS_PIECES = 3
PAGES_PER_STEP = 16
VMEM_LIMIT = 56 * 1024 * 1024

NT_DIMS = (((1,), (1,)), ((), ()))


def _dot(a, b):
    return jnp.dot(a.astype(BF16), b.astype(BF16), preferred_element_type=F32)


def _dot_nt(a, b):
    return lax.dot_general(a.astype(BF16), b.astype(BF16), NT_DIMS, preferred_element_type=F32)


def _dot_split(x, m):
    hi = x.astype(BF16)
    lo = (x - hi.astype(F32)).astype(BF16)
    return (jnp.dot(hi, m, preferred_element_type=F32)
            + jnp.dot(lo, m, preferred_element_type=F32))


def _rms(x, gain):
    return x * lax.rsqrt(jnp.mean(x * x, axis=-1, keepdims=True) + EPS) * gain


def _params(sem):
    return pltpu.CompilerParams(dimension_semantics=sem, vmem_limit_bytes=VMEM_LIMIT)


def _const_spec(shape, single_buffer=True):
    n = len(shape)
    if single_buffer:
        return pl.BlockSpec(shape, lambda *_: (0,) * n, pipeline_mode=pl.Buffered(1))
    return pl.BlockSpec(shape, lambda *_: (0,) * n)


def _proj_in_kernel(*refs, wa, wr, ws, n_prev):
    h_ref, g_ref, w_ref, cos_ref, sin_ref, qg_ref, kg_ref, gm_ref = refs[:8]
    refs = refs[8:]
    if n_prev > 0:
        kprev_ref, vprev_ref = refs[:2]
        refs = refs[2:]
    if n_prev < 0:
        ua_ref, qr_ref, kr_ref, vr_ref, gr_ref, qs_ref, ks_ref, vs_ref = refs
    else:
        ua_ref, qr_ref, kr_ref, vr_ref, gr_ref, qs_ref, vs_ref, kst_ref, vst_ref = refs
    n = _rms(h_ref[...], g_ref[...]).astype(BF16)

    def seg(lo, width):
        return jnp.dot(n, w_ref[:, lo:lo + width], preferred_element_type=F32)

    cos = cos_ref[...]
    sin = sin_ref[...]
    lane = lax.broadcasted_iota(jnp.int32, (1, wr), 1)
    first_half = (lane & (HEAD_DIM - 1)) < HEAD_DIM // 2

    def rope(v):
        swapped = jnp.where(first_half, pltpu.roll(v, wr - HEAD_DIM // 2, 1),
                            pltpu.roll(v, HEAD_DIM // 2, 1))
        return v * cos + swapped * sin

    gm = gm_ref[...]

    def head_rms(v, gain):
        ms = jnp.dot((v * v).astype(BF16), gm, preferred_element_type=F32)
        return v * lax.rsqrt(ms + EPS) * gain

    o = 0
    ua_ref[...] = seg(o, wa)
    o += wa
    qr_ref[...] = rope(seg(o, wr))
    o += wr
    kr_ref[...] = rope(seg(o, wr)) * (HEAD_DIM ** -0.5)
    o += wr
    vr_ref[...] = seg(o, wr)
    o += wr
    gr_ref[...] = seg(o, wr)
    o += wr
    qs_ref[...] = head_rms(seg(o, ws), qg_ref[...]) * (LOG2E * HEAD_DIM ** -0.5)
    o += ws
    ks = head_rms(seg(o, ws), kg_ref[...])
    o += ws
    vs = seg(o, ws)
    vs_ref[...] = vs
    if n_prev < 0:
        ks_ref[...] = ks
    else:
        if n_prev > 0:
            kst_ref[0:n_prev] = kprev_ref[...]
            vst_ref[0:n_prev] = vprev_ref[...]
        kst_ref[n_prev] = ks.T
        vst_ref[n_prev] = vs.T


def _proj_in(h2, gain, w_in, cos_t, sin_t, q_gain, k_gain, gm, kv_prev=None, *, nb, seq, tm, wa, wr,
             ws, n_prev=-1):
    d = h2.shape[1]
    nt = seq // tm
    row = lambda b, t: (b * nt + t, 0)
    tspec = lambda w: pl.BlockSpec((tm, w), row)
    args = [h2, gain, w_in, cos_t, sin_t, q_gain, k_gain, gm]
    in_specs = [tspec(d), _const_spec((1, d)), _const_spec(w_in.shape),
                pl.BlockSpec((tm, wr), lambda b, t: (t, 0)),
                pl.BlockSpec((tm, wr), lambda b, t: (t, 0)),
                _const_spec((1, ws)), _const_spec((1, ws)), _const_spec((ws, ws))]
    out_shape = [jax.ShapeDtypeStruct((seq, nb * wa), F32)]
    out_shape += [jax.ShapeDtypeStruct((nb * seq, wr), F32)] * 4
    out_shape += [jax.ShapeDtypeStruct((nb * seq, ws), F32)] * 2
    out_specs = [pl.BlockSpec((tm, wa), lambda b, t: (t, b))] + [tspec(wr)] * 4 + [tspec(ws)] * 2
    if n_prev < 0:
        out_shape += [jax.ShapeDtypeStruct((nb * seq, ws), F32)]
        out_specs += [tspec(ws)]
    else:
        stack = lambda n: pl.BlockSpec((n, ws, tm), lambda b, t: (0, b, t))
        if n_prev > 0:
            args += list(kv_prev)
            in_specs += [stack(n_prev)] * 2
        out_shape += [jax.ShapeDtypeStruct((n_prev + 1, nb * ws, seq), F32)] * 2
        out_specs += [stack(n_prev + 1)] * 2
    return pl.pallas_call(
        functools.partial(_proj_in_kernel, wa=wa, wr=wr, ws=ws, n_prev=n_prev),
        grid=(nb, nt),
        in_specs=in_specs,
        out_specs=out_specs,
        out_shape=out_shape,
        compiler_params=_params(("arbitrary", "arbitrary")),
        name="proj_in",
    )(*args)


def _s5_kernel(u_ref, s0re_ref, s0im_ref, are_ref, aim_ref, bblk_ref, cblk_ref, dsk_ref,
               gw_ref, gb_ref, y_ref, fre_ref, fim_ref, drv_scr, hre_scr, him_scr, *, nb, tc):
    ns = hre_scr.shape[1]

    @pl.when(pl.program_id(0) == 0)
    def _():
        hre_scr[...] = s0re_ref[...]
        him_scr[...] = s0im_ref[...]

    u = u_ref[...]
    drv_scr[...] = _dot(u, bblk_ref[...])
    are = jnp.broadcast_to(are_ref[...], (nb, ns))
    aim = jnp.broadcast_to(aim_ref[...], (nb, ns))

    def step(t, carry):
        hre, him = carry
        rows = pl.ds(pl.multiple_of(t * nb, nb), nb)
        nre = are * hre - aim * him + drv_scr[rows, 0:ns]
        nim = are * him + aim * hre + drv_scr[rows, ns:2 * ns]
        drv_scr[rows, 0:ns] = nre
        drv_scr[rows, ns:2 * ns] = nim
        return nre, nim

    hre, him = lax.fori_loop(0, tc, step, (hre_scr[...], him_scr[...]))
    hre_scr[...] = hre
    him_scr[...] = him
    fre_ref[...] = hre
    fim_ref[...] = him
    y = _dot(drv_scr[...], cblk_ref[...]) + dsk_ref[...] * u
    gy = jax.nn.gelu(y)
    y_ref[...] = gy * jax.nn.sigmoid(_dot(gy, gw_ref[...]) + gb_ref[...])


def _s5(u_tm, s0re, s0im, are, aim, bblk, cblk, dsk, gw, gb, *, nb, seq, tc):
    wa = u_tm.shape[1]
    ns = are.shape[1]
    rows = tc * nb
    return pl.pallas_call(
        functools.partial(_s5_kernel, nb=nb, tc=tc),
        grid=(seq // tc,),
        in_specs=[pl.BlockSpec((rows, wa), lambda i: (i, 0)),
                  _const_spec((nb, ns)), _const_spec((nb, ns)),
                  _const_spec((1, ns)), _const_spec((1, ns)),
                  _const_spec(bblk.shape), _const_spec(cblk.shape), _const_spec((1, wa)),
                  _const_spec(gw.shape), _const_spec((1, wa))],
        out_specs=[pl.BlockSpec((rows, wa), lambda i: (i, 0)),
                   pl.BlockSpec((nb, ns), lambda i: (0, 0)),
                   pl.BlockSpec((nb, ns), lambda i: (0, 0))],
        out_shape=[jax.ShapeDtypeStruct((seq * nb, wa), F32),
                   jax.ShapeDtypeStruct((nb, ns), F32),
                   jax.ShapeDtypeStruct((nb, ns), F32)],
        scratch_shapes=[pltpu.VMEM((rows, 2 * ns), F32), pltpu.VMEM((nb, ns), F32),
                        pltpu.VMEM((nb, ns), F32)],
        compiler_params=_params(("arbitrary",)),
        name="s5",
    )(u_tm, s0re, s0im, are, aim, bblk, cblk, dsk, gw, gb)


def _head_lane_mask(width, h):
    lane = lax.broadcasted_iota(jnp.int32, (1, width), 1)
    return (lane >> 6) == h


def _ret_intra(q, k, v, dmask_ref, nh):
    wr = q.shape[1]
    o = jnp.zeros(q.shape, F32)
    for h in range(nh):
        mh = _head_lane_mask(wr, h)
        s = _dot_nt(jnp.where(mh, q, 0.0), k) * dmask_ref[h]
        o = jnp.where(mh, _dot(s, v), o)
    return o


def _ret_finish(o, gate, gn_ref, gm_ref):
    gm = gm_ref[...]
    xc = o - _dot_split(o, gm)
    var = _dot_split(xc * xc, gm)
    return jax.nn.silu(gate) * (xc * lax.rsqrt(var + EPS) * gn_ref[...])


def _ret_kernel(q_ref, k_ref, v_ref, g_ref, dmask_ref, xi_ref, zeta_ref, dcol_ref, gn_ref, gm_ref,
                y_ref, sfin_ref, s_scr, *, nh):
    nseq, _, wr = q_ref.shape

    @pl.when(pl.program_id(1) == 0)
    def _():
        s_scr[...] = jnp.zeros(s_scr.shape, F32)

    row_h = lax.broadcasted_iota(jnp.int32, (wr, wr), 0) >> 6
    col_h = lax.broadcasted_iota(jnp.int32, (wr, wr), 1) >> 6
    for i in range(nseq):
        q = q_ref[i]
        k = k_ref[i]
        v = v_ref[i]
        s_prev = s_scr[i]
        o = _ret_intra(q, k, v, dmask_ref, nh) + _dot(q * xi_ref[...], s_prev)
        kv = _dot((k * zeta_ref[...]).T, v)
        s_new = s_prev * dcol_ref[...] + jnp.where(row_h == col_h, kv, 0.0)
        s_scr[i] = s_new
        sfin_ref[i] = s_new
        y_ref[i] = _ret_finish(o, g_ref[i], gn_ref, gm_ref)


def _retention(q, k, v, g, dmask, xi, zeta, dcol, gn, gm, *, nb, seq, chunk, nh, nseq):
    wr = q.shape[-1]
    nc = seq // chunk
    tspec = pl.BlockSpec((nseq, chunk, wr), lambda b, c: (b, c, 0))
    return pl.pallas_call(
        functools.partial(_ret_kernel, nh=nh),
        grid=(nb // nseq, nc),
        in_specs=[tspec] * 4 + [_const_spec(dmask.shape), _const_spec(xi.shape),
                                _const_spec(zeta.shape), _const_spec((1, wr)),
                                _const_spec((1, wr)), _const_spec((wr, wr))],
        out_specs=[tspec, pl.BlockSpec((nseq, wr, wr), lambda b, c: (b, 0, 0))],
        out_shape=[jax.ShapeDtypeStruct((nb, seq, wr), F32),
                   jax.ShapeDtypeStruct((nb, wr, wr), F32)],
        scratch_shapes=[pltpu.VMEM((nseq, wr, wr), F32)],
        compiler_params=_params(("arbitrary", "arbitrary")),
        name="retention",
    )(q, k, v, g, dmask, xi, zeta, dcol, gn, gm)


def _ret_step_kernel(q_ref, k_ref, v_ref, g_ref, dmask_ref, xi_ref, zeta_ref, dcol_ref, gn_ref,
                     gm_ref, s0_ref, exp_ref, y_ref, snew_ref, *, nh, ntok):
    t_rows, wr = q_ref.shape
    sw = s0_ref.shape[0]
    q = q_ref[...]
    k = k_ref[...]
    v = v_ref[...]
    s0 = s0_ref[...]
    o = _ret_intra(q, k, v, dmask_ref, nh)
    qx = q * xi_ref[...]
    kz = k * zeta_ref[...]
    own_b = ((lax.broadcasted_iota(jnp.int32, (t_rows, sw), 1) >> 6)
             == lax.broadcasted_iota(jnp.int32, (t_rows, sw), 0) // ntok)
    s_new = s0 * dcol_ref[...]
    for h in range(nh):
        mh = _head_lane_mask(wr, h)
        q_exp = jnp.where(own_b, _dot(jnp.where(mh, qx, 0.0), exp_ref[h]), 0.0)
        k_exp = jnp.where(own_b, _dot(jnp.where(mh, kz, 0.0), exp_ref[h]), 0.0)
        o = o + jnp.where(mh, _dot(q_exp, s0), 0.0)
        s_new = s_new + jnp.where(mh, _dot(k_exp.T, v), 0.0)
    snew_ref[...] = s_new
    y_ref[...] = _ret_finish(o, g_ref[...], gn_ref, gm_ref)


def _retention_step(q, k, v, g, dmask, xi, zeta, dcol, gn, gm, s0cat, expand, *, nh, ntok):
    t_rows, wr = q.shape
    args = (q, k, v, g, dmask, xi, zeta, dcol, gn, gm, s0cat, expand)
    return pl.pallas_call(
        functools.partial(_ret_step_kernel, nh=nh, ntok=ntok),
        grid=(1,),
        in_specs=[_const_spec(a.shape) for a in args],
        out_specs=[_const_spec((t_rows, wr), False), _const_spec(s0cat.shape, False)],
        out_shape=[jax.ShapeDtypeStruct((t_rows, wr), F32),
                   jax.ShapeDtypeStruct(s0cat.shape, F32)],
        compiler_params=_params(("arbitrary",)),
        name="retention_step",
    )(*args)


def _sb_fail(z2, causal):
    g = jnp.maximum(z2, 0.0) + jnp.log(1.0 + jnp.exp2(-jnp.abs(z2))) * LOG2E
    return g if causal is None else jnp.where(causal, g, 0.0)


def _sb_weight(z2, suffix, run, causal):
    w = jnp.exp2(z2 - suffix - run)
    return w if causal is None else jnp.where(causal, w, 0.0)


def _sb_kernel(ext_ref, q_ref, k_ref, v_ref, u_ref, o_ref, acc_scr, run_scr, *, tq):
    qi = pl.program_id(2)
    q = q_ref[...]
    width = q.shape[1]
    nh = width // HEAD_DIM
    rows = nh * tq
    umat = u_ref[...]
    q2 = jnp.concatenate([jnp.where(_head_lane_mask(width, h), q, 0.0) for h in range(nh)],
                         axis=0).astype(BF16)
    q2 = jnp.concatenate([q2, ext_ref[...]], axis=1)
    ones = (lax.broadcasted_iota(jnp.int32, (LANES, tq), 0) < BIAS_PIECES).astype(BF16)
    causal = (lax.broadcasted_iota(jnp.int32, (rows, tq), 1)
              < (lax.broadcasted_iota(jnp.int32, (rows, tq), 0) & (tq - 1)))

    def tiles(j, count, diagonal):
        acc = acc_scr[...]
        run = run_scr[...]
        masks = [causal if diagonal else None] + [None] * (count - 1)
        keys = [pl.ds(pl.multiple_of((j - i) * tq, tq), tq) for i in range(count)]
        zs = [jnp.dot(q2, jnp.concatenate([k_ref[:, ks].astype(BF16), ones], axis=0),
                      preferred_element_type=F32) for ks in keys]
        suffixes = [_dot(_sb_fail(z, m), umat) for z, m in zip(zs, masks)]
        for z, suffix, ks, m in zip(zs, suffixes, keys, masks):
            acc = acc + _dot(_sb_weight(z, suffix, run, m), v_ref[ks, :])
            run = run + suffix[:, 0:1]
        acc_scr[...] = acc
        run_scr[...] = run

    acc_scr[...] = jnp.zeros(acc_scr.shape, F32)
    run_scr[...] = jnp.zeros(run_scr.shape, F32)
    ntile = qi + 1
    top = qi
    done = jnp.int32(0)
    for bit, count in ((0, 1), (1, 2), (2, 4)):
        has = (ntile >> bit) & 1
        for diagonal in (True, False):
            @pl.when((has == 1) & ((done == 0) == diagonal))
            def _(top=top, count=count, diagonal=diagonal):
                tiles(top, count, diagonal)

        top = top - has * count
        done = done + has

    @pl.when((ntile >= 8) & (done == 0))
    def _():
        tiles(top, 8, True)

    first8 = jnp.where(done == 0, 1, 0)

    @pl.loop(first8, ntile >> 3)
    def _(n):
        tiles(top - 8 * n, 8, False)

    acc = acc_scr[...]
    out = acc[0:tq]
    for h in range(1, nh):
        out = jnp.where(_head_lane_mask(width, h), acc[h * tq:(h + 1) * tq], out)
    o_ref[...] = out


def _stick_breaking(q, kt, v, ext, umat, *, layer, nb, seq, tq):
    ws = q.shape[1]
    nq = seq // tq
    nhp = ws // LANES
    qspec = pl.BlockSpec((tq, LANES), lambda b, hp, i: (b * nq + i, hp))
    kspec = pl.BlockSpec((None, LANES, seq), lambda b, hp, i: (layer, b * nhp + hp, 0))
    return pl.pallas_call(
        functools.partial(_sb_kernel, tq=tq),
        grid=(nb, nhp, nq),
        in_specs=[pl.BlockSpec((None,) + ext.shape[1:], lambda b, hp, i: (hp, 0, 0)), qspec, kspec,
                  pl.BlockSpec((seq, LANES), lambda b, hp, i: (b, hp)), _const_spec((tq, tq))],
        out_specs=qspec,
        out_shape=jax.ShapeDtypeStruct((nb * seq, ws), F32),
        scratch_shapes=[pltpu.VMEM((ext.shape[1], LANES), F32), pltpu.VMEM((ext.shape[1], 1), F32)],
        compiler_params=_params(("arbitrary", "arbitrary", "arbitrary")),
        name="stick_breaking",
    )(ext, q, kt, v, umat)


def _bias_pieces(bias, tq):
    left = bias.astype(F32) * LOG2E
    parts = []
    for _ in range(BIAS_PIECES):
        part = left.astype(BF16)
        parts.append(part)
        left = left - part.astype(F32)
    pieces = jnp.stack(parts, axis=1)
    pieces = jnp.pad(pieces, ((0, 0), (0, LANES - BIAS_PIECES)))
    nh = LANES // HEAD_DIM
    return jnp.repeat(pieces, tq, axis=0).reshape(bias.shape[0] // nh, nh * tq, LANES)


def _sb_step_kernel(pt_ref, q_ref, kn_ref, vn_ref, bias_ref, u_ref, *rest, pp, nh):
    k_pages = rest[:pp]
    v_pages = rest[pp:2 * pp]
    o_ref = rest[2 * pp]
    acc_scr, run_scr = rest[2 * pp + 1:]
    j = pl.program_id(1)
    q = q_ref[...].astype(BF16)
    rows, width = q.shape
    bias = bias_ref[...] * LOG2E
    umat = u_ref[...]

    def sweep(z_pages, causal):
        n = len(z_pages)
        z = jnp.concatenate(z_pages, axis=0) + jnp.concatenate([bias] * n, axis=0)
        suffix = _dot(_sb_fail(z, causal), umat)
        run = run_scr[...]
        later = [None] * n
        for p in reversed(range(n)):
            later[p] = run
            run = run + suffix[p * rows:(p + 1) * rows, 0:1]
        w = _sb_weight(z, suffix, jnp.concatenate(later, axis=0), causal)
        run_scr[...] = run
        return [w[p * rows:(p + 1) * rows] for p in range(n)]

    @pl.when(j == 0)
    def _():
        run_scr[...] = jnp.zeros(run_scr.shape, F32)
        pad = jnp.zeros((PAGE_SIZE - kn_ref.shape[0], width), F32)
        kn = jnp.concatenate([kn_ref[...], pad], axis=0)
        vn = jnp.concatenate([vn_ref[...], pad], axis=0)
        row_t = lax.broadcasted_iota(jnp.int32, (rows, PAGE_SIZE), 0) // nh
        col = lax.broadcasted_iota(jnp.int32, (rows, PAGE_SIZE), 1)
        (w,) = sweep([_dot_nt(q, kn)], col < row_t)
        acc_scr[...] = _dot(w, vn)

    @pl.when(j > 0)
    def _():
        kt = jnp.concatenate([k[...].astype(BF16) for k in k_pages], axis=1)
        z = jnp.dot(q, kt, preferred_element_type=F32)
        w = sweep([z[:, p * PAGE_SIZE:(p + 1) * PAGE_SIZE] for p in range(pp)], None)
        vt = jnp.concatenate([v[...].astype(BF16) for v in v_pages], axis=1)
        acc_scr[...] += _dot_nt(jnp.concatenate(w, axis=1), vt)

    @pl.when(j == pl.num_programs(1) - 1)
    def _():
        lane_h = lax.broadcasted_iota(jnp.int32, (rows, width), 1) >> 6
        row_h = lax.broadcasted_iota(jnp.int32, (rows, width), 0) % nh
        own = jnp.where(lane_h == row_h, acc_scr[...], 0.0)
        o_ref[...] = jnp.sum(own.reshape(rows // nh, nh, width), axis=1)


def _stick_breaking_step(page_table, q_exp, k_new, v_new, bias_rows, umat, cache_k, cache_v, *,
                         layer, pp, nh):
    nb, rows, width = q_exp.shape
    n_pages = page_table.shape[1]
    n_steps = n_pages // pp
    ntok = rows // nh

    def page_map(slot):
        def index(b, j, pt):
            return (layer, pt[b, n_pages - jnp.maximum(j, 1) * pp + slot], 0, 0)
        return index

    page_specs = [pl.BlockSpec((None, None, width, PAGE_SIZE), page_map(s)) for s in range(pp)]
    per_b = lambda r: pl.BlockSpec((None, r, width), lambda b, j, pt: (b, 0, 0))
    new_spec = per_b(k_new.shape[1])
    grid_spec = pltpu.PrefetchScalarGridSpec(
        num_scalar_prefetch=1,
        grid=(nb, n_steps + 1),
        in_specs=[per_b(rows), new_spec, new_spec,
                  pl.BlockSpec((rows, 1), lambda b, j, pt: (0, 0)),
                  pl.BlockSpec((PAGE_SIZE, PAGE_SIZE), lambda b, j, pt: (0, 0))]
        + page_specs + page_specs,
        out_specs=per_b(ntok),
        scratch_shapes=[pltpu.VMEM((rows, width), F32), pltpu.VMEM((rows, 1), F32)],
    )
    return pl.pallas_call(
        functools.partial(_sb_step_kernel, pp=pp, nh=nh),
        grid_spec=grid_spec,
        out_shape=jax.ShapeDtypeStruct((nb, ntok, width), F32),
        compiler_params=_params(("arbitrary", "arbitrary")),
        name="stick_breaking_step",
    )(page_table, q_exp, k_new, v_new, bias_rows, umat, *([cache_k] * pp), *([cache_v] * pp))


def _mix_ffn_kernel(*refs, tm, wa, wr, decode_tokens):
    if decode_tokens:
        (h_ref, ya_ref, yr_ref, ys_ref, p_ref, mg_ref, wout_ref, nf_ref, wup_ref, wgate_ref,
         cw_ref, cb_ref, wdown_ref, pn_ref, pgate_ref, pproj_ref, prev1_ref, prev2_ref,
         out_ref, conv_ref, ext_scr, acc_scr) = refs
    else:
        (h_ref, ya_ref, yr_ref, ys_ref, p_ref, mg_ref, wout_ref, nf_ref, wup_ref, wgate_ref,
         cw_ref, cb_ref, wdown_ref, pn_ref, pgate_ref, pproj_ref,
         out_ref, conv_ref, ext_scr, acc_scr, carry_scr) = refs

        @pl.when(pl.program_id(1) == 0)
        def _():
            carry_scr[...] = jnp.zeros(carry_scr.shape, F32)

    mg = mg_ref[...]
    a1 = wa + wr
    merged = (_dot(_rms(ya_ref[...], mg[:, 0:wa]), wout_ref[0:wa, :])
              + _dot(_rms(yr_ref[...], mg[:, wa:a1]), wout_ref[wa:a1, :])
              + _dot(_rms(ys_ref[...], mg[:, a1:]), wout_ref[a1:, :]))
    h_mid = h_ref[...] + merged
    n2 = _rms(h_mid, nf_ref[...]).astype(BF16)
    acc_scr[...] = jnp.zeros(acc_scr.shape, F32)
    if decode_tokens:
        ext_scr[0:SUBLANES, :] = jnp.zeros((SUBLANES, FF_CHUNK), F32)
        tok = lax.broadcasted_iota(jnp.int32, (tm, FF_CHUNK), 0) % decode_tokens

    acts = []
    nchunk = wup_ref.shape[1] // FF_CHUNK
    ngroup = -(-nchunk // DOWN_GROUP)
    group_ends = {(i + 1) * nchunk // ngroup for i in range(ngroup)}
    for lo in range(0, wup_ref.shape[1], FF_CHUNK):
        cols = slice(lo, lo + FF_CHUNK)
        u = jnp.dot(n2, wup_ref[:, cols], preferred_element_type=F32)
        g = jnp.dot(n2, wgate_ref[:, cols], preferred_element_type=F32)
        if not decode_tokens:
            ext_scr[0:SUBLANES, :] = carry_scr[:, cols]
        ext_scr[SUBLANES:SUBLANES + tm, :] = u
        prev1 = ext_scr[pl.ds(SUBLANES - 1, tm), :]
        prev2 = ext_scr[pl.ds(SUBLANES - 2, tm), :]
        if decode_tokens:
            prev1 = jnp.where(tok >= 1, prev1, prev1_ref[:, cols])
            prev2 = jnp.where(tok >= 2, prev2, prev2_ref[:, cols])
            conv_ref[:, cols] = u
        else:
            last = u[tm - SUBLANES:tm, :]
            carry_scr[:, cols] = last
            conv_ref[:, cols] = last
        conv = cb_ref[:, cols] + cw_ref[0:1, cols] * prev2
        conv = conv + cw_ref[1:2, cols] * prev1
        conv = conv + cw_ref[2:3, cols] * u
        acts.append((jax.nn.gelu(conv) * g).astype(BF16))
        if (lo + FF_CHUNK) // FF_CHUNK in group_ends:
            width = len(acts) * FF_CHUNK
            acc_scr[...] += jnp.dot(jnp.concatenate(acts, axis=1),
                                    wdown_ref[lo + FF_CHUNK - width:lo + FF_CHUNK, :],
                                    preferred_element_type=F32)
            acts = []
    h2 = h_mid + acc_scr[...]
    gate = jax.nn.sigmoid(_dot(_rms(h2, pn_ref[...]), pgate_ref[...]))
    out_ref[...] = h2 + _dot(p_ref[...], pproj_ref[...]) * gate


def _mix_ffn(h2, ya, yr, ys, p2, w, prev=None, *, nb, seq, tm, decode_tokens=0):
    d = h2.shape[1]
    wa, wr, ws = w["d_skip"].shape[1], yr.shape[1], ys.shape[1]
    pd = p2.shape[1]
    nt = seq // tm
    dff = w["w_up"].shape[1]
    row = lambda b, t: (b * nt + t, 0)
    tspec = lambda width: pl.BlockSpec((tm, width), row)
    ya_spec = tspec(wa) if decode_tokens else pl.BlockSpec((tm, wa), lambda b, t: (t, b))
    weights = [w["merge"], w["w_out"], w["norm_ffn"], w["w_up"], w["w_gate"], w["conv_w"],
               w["conv_b"], w["w_down"], w["ple_norm"], w["ple_gate"], w["ple_proj"]]
    in_specs = [tspec(d), ya_spec, tspec(wr), tspec(ws), tspec(pd)]
    in_specs += [_const_spec(a.shape) for a in weights]
    args = [h2, ya, yr, ys, p2] + weights
    scratch = [pltpu.VMEM((tm + SUBLANES, FF_CHUNK), F32), pltpu.VMEM((tm, d), F32)]
    if decode_tokens:
        in_specs += [_const_spec(prev[0].shape), _const_spec(prev[1].shape)]
        args += list(prev)
        conv_rows = tm
    else:
        scratch.append(pltpu.VMEM((SUBLANES, dff), F32))
        conv_rows = SUBLANES
    return pl.pallas_call(
        functools.partial(_mix_ffn_kernel, tm=tm, wa=wa, wr=wr, decode_tokens=decode_tokens),
        grid=(nb, nt),
        in_specs=in_specs,
        out_specs=[tspec(d),
                   pl.BlockSpec((None, conv_rows, dff), lambda b, t: (b, 0, 0))],
        out_shape=[jax.ShapeDtypeStruct((nb * seq, d), F32),
                   jax.ShapeDtypeStruct((nb, conv_rows, dff), F32)],
        scratch_shapes=scratch,
        compiler_params=_params(("arbitrary", "arbitrary")),
        name="mix_ffn_step" if decode_tokens else "mix_ffn",
    )(*args)


def _block_diag(blocks):
    g, r, c = blocks.shape
    eye = jnp.eye(g, dtype=blocks.dtype)
    return (eye[:, None, :, None] * blocks[:, :, None, :]).reshape(g * r, g * c)


def _s5_params(lam_re, lam_im, log_dt, b_re, b_im, c_re, c_im):
    dt = jnp.exp(log_dt.astype(F32))[:, None]
    lr = lam_re.astype(F32)
    li = lam_im.astype(F32)
    mag = jnp.exp(lr * dt)
    ang = li * dt
    ab_re = mag * jnp.cos(ang)
    ab_im = mag * jnp.sin(ang)
    den = lr * lr + li * li
    f_re = ((ab_re - 1.0) * lr + ab_im * li) / den
    f_im = (ab_im * lr - (ab_re - 1.0) * li) / den
    bb_re = f_re[..., None] * b_re - f_im[..., None] * b_im
    bb_im = f_re[..., None] * b_im + f_im[..., None] * b_re
    to_in = lambda m: _block_diag(jnp.swapaxes(m, 1, 2))
    bblk = jnp.concatenate([to_in(bb_re), to_in(bb_im)], axis=1)
    to_out = lambda m: _block_diag(jnp.swapaxes(m, 1, 2))
    cblk = jnp.concatenate([to_out(c_re), -to_out(c_im)], axis=0)
    return ab_re.reshape(1, -1), ab_im.reshape(1, -1), bblk.astype(BF16), cblk.astype(BF16)


def _ret_consts(nh, chunk, group):
    log_g = jnp.log1p(-jnp.exp2(-5.0 - jnp.arange(nh, dtype=F32)))
    idx = jnp.arange(chunk, dtype=jnp.int32)
    pos = (idx % group).astype(F32)
    diff = pos[:, None] - pos[None, :]
    ok = (diff >= 0) & ((idx // group)[:, None] == (idx // group)[None, :])
    dmask = jnp.where(ok[None], jnp.exp(jnp.where(ok, diff, 0.0)[None] * log_g[:, None, None]), 0.0)
    xi = jnp.exp((pos + 1.0)[None, :] * log_g[:, None])
    zeta = jnp.exp((group - 1.0 - pos)[None, :] * log_g[:, None])
    d_chunk = jnp.exp(group * log_g)
    lanes = lambda t: jnp.repeat(t.T, HEAD_DIM, axis=1)
    return dmask, lanes(xi), lanes(zeta), jnp.repeat(d_chunk, HEAD_DIM)[None, :]


def _rope_tables(pos, nh):
    half = HEAD_DIM // 2
    inv = jnp.power(ROPE_BASE, -jnp.arange(half, dtype=F32) / half)
    ang = pos.astype(F32)[:, None] * inv[None, :]
    cos = jnp.cos(ang)
    sin = jnp.sin(ang)
    return (jnp.tile(jnp.concatenate([cos, cos], axis=1), (1, nh)),
            jnp.tile(jnp.concatenate([-sin, sin], axis=1), (1, nh)))


def _group_mean_matrix(width):
    g = jnp.arange(width) // HEAD_DIM
    return jnp.where(g[:, None] == g[None, :], 1.0 / HEAD_DIM, 0.0).astype(BF16)


def _suffix_matrix(n):
    i = jnp.arange(n)
    return (i[:, None] >= i[None, :]).astype(BF16)


def _layer_weights(i, wts):
    (norm_mix_gain, w_in, lam_re, lam_im, log_dt, b_re, b_im, c_re, c_im, s5_d, glu_w, glu_b,
     ret_gn, q_gain, k_gain, sb_bias, merge, w_out, norm_ffn, w_up, w_gate, conv_w, conv_b, w_down,
     ple_norm, ple_gate, ple_proj) = [t[i] for t in wts]
    nh_s = sb_bias.shape[0]
    row = lambda v: v.reshape(1, -1)
    are, aim, bblk, cblk = _s5_params(lam_re, lam_im, log_dt, b_re, b_im, c_re, c_im)
    conv_w8 = jnp.concatenate([conv_w, jnp.zeros((SUBLANES - CONV_W, conv_w.shape[1]), F32)], axis=0)
    return dict(
        norm_mix=row(norm_mix_gain), w_in=w_in.astype(BF16),
        are=are, aim=aim, bblk=bblk, cblk=cblk, d_skip=row(s5_d), glu_w=glu_w.astype(BF16),
        glu_b=row(glu_b), ret_gn=row(ret_gn),
        q_gain=row(jnp.tile(q_gain, nh_s)), k_gain=row(jnp.tile(k_gain, nh_s)), sb_bias=sb_bias,
        merge=row(merge), w_out=w_out.astype(BF16), norm_ffn=row(norm_ffn),
        w_up=w_up.astype(BF16), w_gate=w_gate.astype(BF16), conv_w=conv_w8, conv_b=row(conv_b),
        w_down=w_down.astype(BF16),
        ple_norm=row(ple_norm), ple_gate=ple_gate.astype(BF16), ple_proj=ple_proj.astype(BF16))


def _prompt_layer(h2, p2, w, consts, kv_prev, *, layer, nb, seq):
    wa, wr, ws, nh_r = consts["wa"], consts["wr"], consts["ws"], consts["nh_r"]
    tm, tq, chunk, tc = consts["tm"], consts["tq"], consts["chunk"], consts["tc"]
    ua, qr, kr, vr, gr, qs, vs, kst, vst = _proj_in(
        h2, w["norm_mix"], w["w_in"], consts["cos"], consts["sin"], w["q_gain"], w["k_gain"],
        consts["gm_s"], kv_prev, nb=nb, seq=seq, tm=tm, wa=wa, wr=wr, ws=ws, n_prev=layer)
    zeros = jnp.zeros((nb, w["are"].shape[1]), F32)
    ya, sre, sim = _s5(ua.reshape(seq * nb, wa), zeros, zeros, w["are"], w["aim"], w["bblk"],
                       w["cblk"], w["d_skip"], w["glu_w"], w["glu_b"], nb=nb, seq=seq, tc=tc)
    seq3 = lambda a: a.reshape(nb, seq, wr)
    yr, sret = _retention(seq3(qr), seq3(kr), seq3(vr), seq3(gr), consts["dmask"], consts["xi"],
                          consts["zeta"], consts["dcol"], w["ret_gn"], consts["gm_r"], nb=nb,
                          seq=seq, chunk=chunk, nh=nh_r, nseq=consts["ret_nseq"])
    yr = yr.reshape(nb * seq, wr)
    ys = _stick_breaking(qs, kst, vs, _bias_pieces(w["sb_bias"], tq), consts["later_q"],
                         layer=layer, nb=nb, seq=seq, tq=tq)
    h_out, conv = _mix_ffn(h2, ya.reshape(seq, nb * wa), yr, ys, p2, w, nb=nb, seq=seq, tm=tm)
    ret_state = jnp.stack([sret[:, h * HEAD_DIM:(h + 1) * HEAD_DIM, h * HEAD_DIM:(h + 1) * HEAD_DIM]
                           for h in range(nh_r)], axis=1)
    conv_state = conv[:, SUBLANES - (CONV_W - 1):, :]
    return h_out, (kst, vst), (sre, sim, ret_state, conv_state)


def _decode_layer(h2, p2, w, consts, past, *, layer, nb, ntok):
    wa, wr, ws, nh_r, nh_s = consts["wa"], consts["wr"], consts["ws"], consts["nh_r"], consts["nh_s"]
    rows = nb * ntok
    ua, qr, kr, vr, gr, qs, ks, vs = _proj_in(
        h2, w["norm_mix"], w["w_in"], consts["cos"], consts["sin"], w["q_gain"], w["k_gain"],
        consts["gm_s"], nb=1, seq=rows, tm=rows, wa=wa, wr=wr, ws=ws)
    to_tm = lambda a: a.reshape(nb, ntok, -1).transpose(1, 0, 2).reshape(rows, -1)
    from_tm = lambda a: a.reshape(ntok, nb, -1).transpose(1, 0, 2).reshape(rows, -1)
    ya, sre, sim = _s5(to_tm(ua), past["ssm_re"], past["ssm_im"], w["are"], w["aim"], w["bblk"],
                       w["cblk"], w["d_skip"], w["glu_w"], w["glu_b"], nb=nb, seq=ntok, tc=ntok)
    s0cat = past["ret"].transpose(0, 2, 1, 3).reshape(nb * HEAD_DIM, wr)
    yr, snew = _retention_step(qr, kr, vr, gr, consts["dmask"], consts["xi"], consts["zeta"],
                               consts["dcol"], w["ret_gn"], consts["gm_r"], s0cat,
                               consts["expand"], nh=nh_r, ntok=ntok)
    ret_state = snew.reshape(nb, HEAD_DIM, nh_r, HEAD_DIM).transpose(0, 2, 1, 3)
    q_rep = jnp.repeat(qs.reshape(nb, ntok, ws), nh_s, axis=1)
    q_exp = jnp.where(consts["q_own"], q_rep, 0.0)
    pad = lambda a: jnp.pad(a.reshape(nb, ntok, ws), ((0, 0), (0, -ntok % SUBLANES), (0, 0)))
    bias_rows = jnp.tile(w["sb_bias"], ntok).reshape(ntok * nh_s, 1)
    ys = _stick_breaking_step(past["page_table"], q_exp, pad(ks), pad(vs), bias_rows,
                              consts["later_p"], past["cache_k"], past["cache_v"], layer=layer,
                              pp=consts["pp"], nh=nh_s).reshape(rows, ws)
    buf = past["conv"]
    zero = jnp.zeros_like(buf[:, :1])
    prev1 = jnp.concatenate([buf[:, 1:2]] + [zero] * (ntok - 1), axis=1).reshape(rows, -1)
    prev2 = jnp.concatenate([buf[:, 0:1], buf[:, 1:2]] + [zero] * (ntok - 2), axis=1).reshape(rows, -1)
    h_out, conv = _mix_ffn(h2, from_tm(ya), yr, ys, p2, w, (prev1, prev2),
                           nb=1, seq=rows, tm=rows, decode_tokens=ntok)
    conv_state = conv.reshape(nb, ntok, -1)[:, ntok - (CONV_W - 1):, :]
    return h_out, (ks, vs, sre, sim, ret_state, conv_state)


def kernel(x_prompt, x_sample, cache_k, cache_v, state_ssm_re, state_ssm_im, state_ret, state_conv, page_table, p_prompt, p_sample, norm_mix_gain, w_in, s5_lambda_re, s5_lambda_im, s5_log_dt, s5_b_re, s5_b_im, s5_c_re, s5_c_im, s5_d, s5_glu_w, s5_glu_b, ret_gn_gain, sb_q_gain, sb_k_gain, sb_logit_bias, merge_gain, w_out, norm_ffn_gain, ffn_w_up, ffn_w_gate, ffn_conv_w, ffn_conv_b, ffn_w_down, ple_norm_gain, ple_w_gate, ple_w_proj):
    wts = (norm_mix_gain, w_in, s5_lambda_re, s5_lambda_im, s5_log_dt, s5_b_re, s5_b_im, s5_c_re,
           s5_c_im, s5_d, s5_glu_w, s5_glu_b, ret_gn_gain, sb_q_gain, sb_k_gain, sb_logit_bias,
           merge_gain, w_out, norm_ffn_gain, ffn_w_up, ffn_w_gate, ffn_conv_w, ffn_conv_b,
           ffn_w_down, ple_norm_gain, ple_w_gate, ple_w_proj)
    depth = w_in.shape[0]
    nb, seq, d = x_prompt.shape
    nbs, ntok, _ = x_sample.shape
    past_len = page_table.shape[1] * PAGE_SIZE
    wa = s5_d.shape[1]
    wr = ret_gn_gain.shape[1]
    nh_r = wr // HEAD_DIM
    nh_s = sb_logit_bias.shape[1]
    ws = nh_s * HEAD_DIM
    n_pool = cache_k.shape[1]
    rows_s = nbs * ntok

    shared = dict(wa=wa, wr=wr, ws=ws, nh_r=nh_r, nh_s=nh_s, gm_s=_group_mean_matrix(ws),
                  gm_r=_group_mean_matrix(wr))
    chunk = min(128, seq)
    cos_p, sin_p = _rope_tables(jnp.arange(seq, dtype=jnp.int32), nh_r)
    dmask_p, xi_p, zeta_p, dcol_p = _ret_consts(nh_r, chunk, chunk)
    tq = min(256, seq)
    consts_p = dict(shared, cos=cos_p, sin=sin_p, dmask=dmask_p, xi=xi_p, zeta=zeta_p, dcol=dcol_p,
                    tm=min(512, seq), tq=tq, chunk=chunk, tc=min(128, seq),
                    later_q=_suffix_matrix(tq), ret_nseq=RET_SEQS if nb % RET_SEQS == 0 else 1)
    pos_s = past_len + jnp.tile(jnp.arange(ntok, dtype=jnp.int32), nbs)
    cos_s, sin_s = _rope_tables(pos_s, nh_r)
    dmask_s, xi_s, zeta_s, dcol_s = _ret_consts(nh_r, rows_s, ntok)
    lane_src = jnp.arange(wr)
    lane_dst = jnp.arange(nbs * HEAD_DIM)
    expand = jnp.stack([(lane_src[:, None] == h * HEAD_DIM + lane_dst[None, :] % HEAD_DIM)
                        for h in range(nh_r)]).astype(BF16)
    q_own = (jnp.arange(ws)[None, :] // HEAD_DIM) == (jnp.arange(ntok * nh_s)[:, None] % nh_s)
    consts_s = dict(shared, cos=cos_s, sin=sin_s, dmask=dmask_s, xi=xi_s, zeta=zeta_s, dcol=dcol_s,
                    expand=expand, q_own=q_own[None], later_p=_suffix_matrix(PAGE_SIZE),
                    pp=min(PAGES_PER_STEP, page_table.shape[1]))

    ck = cache_k.transpose(0, 1, 3, 4, 2).reshape(depth, n_pool, ws, PAGE_SIZE)
    cv = cache_v.transpose(0, 1, 3, 4, 2).reshape(depth, n_pool, ws, PAGE_SIZE)
    h_p = x_prompt.reshape(nb * seq, d)
    h_s = x_sample.reshape(rows_s, d)
    outs_p, outs_s = [], []
    kv_p = None
    for i in range(depth):
        w = _layer_weights(i, wts)
        h_p, kv_p, st_p = _prompt_layer(h_p, p_prompt[i].reshape(nb * seq, -1), w, consts_p, kv_p,
                                        layer=i, nb=nb, seq=seq)
        past = dict(page_table=page_table, cache_k=ck, cache_v=cv,
                    ssm_re=state_ssm_re[i].reshape(nbs, -1), ssm_im=state_ssm_im[i].reshape(nbs, -1),
                    ret=state_ret[i], conv=state_conv[i])
        h_s, st_s = _decode_layer(h_s, p_sample[i].reshape(rows_s, -1), w, consts_s, past,
                                  layer=i, nb=nbs, ntok=ntok)
        outs_p.append(st_p)
        outs_s.append(st_s)

    st_shape = lambda b: (depth, b, wa // S5_CH, S5_STATE)
    stack = lambda outs: [jnp.stack(x, axis=0) for x in zip(*outs)]
    k_p, v_p = [t.reshape(depth, nb, nh_s, HEAD_DIM, seq).transpose(0, 1, 4, 2, 3) for t in kv_p]
    sre_p, sim_p, ret_p, conv_p = stack(outs_p)
    k_s, v_s, sre_s, sim_s, ret_s, conv_s = stack(outs_s)
    kv_shape_s = (depth, nbs, ntok, nh_s, HEAD_DIM)
    return (h_p.reshape(nb, seq, d), h_s.reshape(nbs, ntok, d),
            k_p, v_p, sre_p.reshape(st_shape(nb)), sim_p.reshape(st_shape(nb)), ret_p, conv_p,
            k_s.reshape(kv_shape_s), v_s.reshape(kv_shape_s), sre_s.reshape(st_shape(nbs)),
            sim_s.reshape(st_shape(nbs)), ret_s, conv_s)
```

```python
import functools

import jax
import jax.numpy as jnp
from jax import lax
from jax.experimental import pallas as pl
from jax.experimental.pallas import tpu as pltpu

F32 = jnp.float32
BF16 = jnp.bfloat16

EPS = 1e-6
LOG2E = 1.4426950408889634
ROPE_BASE = 10000.0
HEAD_DIM = 64
S5_CH = 16
S5_STATE = 64
PAGE_SIZE = 128
CONV_W = 3
LANES = 128
SUBLANES = 8
FF_CHUNK = 256
DOWN_GROUP = 4
RET_SEQS = 4
BIAS_PIECES = 3
PAGES_PER_STEP = 16
VMEM_LIMIT = 56 * 1024 * 1024

NT_DIMS = (((1,), (1,)), ((), ()))


def _dot(a, b):
    return jnp.dot(a.astype(BF16), b.astype(BF16), preferred_element_type=F32)


def _dot_nt(a, b):
    return lax.dot_general(a.astype(BF16), b.astype(BF16), NT_DIMS, preferred_element_type=F32)


def _dot_split(x, m):
    hi = x.astype(BF16)
    lo = (x - hi.astype(F32)).astype(BF16)
    return (jnp.dot(hi, m, preferred_element_type=F32)
            + jnp.dot(lo, m, preferred_element_type=F32))


def _rms(x, gain):
    return x * lax.rsqrt(jnp.mean(x * x, axis=-1, keepdims=True) + EPS) * gain


def _params(sem):
    return pltpu.CompilerParams(dimension_semantics=sem, vmem_limit_bytes=VMEM_LIMIT)


def _const_spec(shape, single_buffer=True):
    n = len(shape)
    if single_buffer:
        return pl.BlockSpec(shape, lambda *_: (0,) * n, pipeline_mode=pl.Buffered(1))
    return pl.BlockSpec(shape, lambda *_: (0,) * n)


def _proj_in_kernel(*refs, wa, wr, ws, n_prev):
    h_ref, g_ref, w_ref, cos_ref, sin_ref, qg_ref, kg_ref, gm_ref = refs[:8]
    refs = refs[8:]
    if n_prev > 0:
        kprev_ref, vprev_ref = refs[:2]
        refs = refs[2:]
    if n_prev < 0:
        ua_ref, qr_ref, kr_ref, vr_ref, gr_ref, qs_ref, ks_ref, vs_ref = refs
    else:
        ua_ref, qr_ref, kr_ref, vr_ref, gr_ref, qs_ref, vs_ref, kst_ref, vst_ref = refs
    n = _rms(h_ref[...], g_ref[...]).astype(BF16)

    def seg(lo, width):
        return jnp.dot(n, w_ref[:, lo:lo + width], preferred_element_type=F32)

    cos = cos_ref[...]
    sin = sin_ref[...]
    lane = lax.broadcasted_iota(jnp.int32, (1, wr), 1)
    first_half = (lane & (HEAD_DIM - 1)) < HEAD_DIM // 2

    def rope(v):
        swapped = jnp.where(first_half, pltpu.roll(v, wr - HEAD_DIM // 2, 1),
                            pltpu.roll(v, HEAD_DIM // 2, 1))
        return v * cos + swapped * sin

    gm = gm_ref[...]

    def head_rms(v, gain):
        ms = jnp.dot((v * v).astype(BF16), gm, preferred_element_type=F32)
        return v * lax.rsqrt(ms + EPS) * gain

    o = 0
    ua_ref[...] = seg(o, wa)
    o += wa
    qr_ref[...] = rope(seg(o, wr))
    o += wr
    kr_ref[...] = rope(seg(o, wr)) * (HEAD_DIM ** -0.5)
    o += wr
    vr_ref[...] = seg(o, wr)
    o += wr
    gr_ref[...] = seg(o, wr)
    o += wr
    qs_ref[...] = head_rms(seg(o, ws), qg_ref[...]) * (LOG2E * HEAD_DIM ** -0.5)
    o += ws
    ks = head_rms(seg(o, ws), kg_ref[...])
    o += ws
    vs = seg(o, ws)
    vs_ref[...] = vs
    if n_prev < 0:
        ks_ref[...] = ks
    else:
        if n_prev > 0:
            kst_ref[0:n_prev] = kprev_ref[...]
            vst_ref[0:n_prev] = vprev_ref[...]
        kst_ref[n_prev] = ks.T
        vst_ref[n_prev] = vs.T


def _proj_in(h2, gain, w_in, cos_t, sin_t, q_gain, k_gain, gm, kv_prev=None, *, nb, seq, tm, wa, wr,
             ws, n_prev=-1):
    d = h2.shape[1]
    nt = seq // tm
    row = lambda b, t: (b * nt + t, 0)
    tspec = lambda w: pl.BlockSpec((tm, w), row)
    args = [h2, gain, w_in, cos_t, sin_t, q_gain, k_gain, gm]
    in_specs = [tspec(d), _const_spec((1, d)), _const_spec(w_in.shape),
                pl.BlockSpec((tm, wr), lambda b, t: (t, 0)),
                pl.BlockSpec((tm, wr), lambda b, t: (t, 0)),
                _const_spec((1, ws)), _const_spec((1, ws)), _const_spec((ws, ws))]
    out_shape = [jax.ShapeDtypeStruct((seq, nb * wa), F32)]
    out_shape += [jax.ShapeDtypeStruct((nb * seq, wr), F32)] * 4
    out_shape += [jax.ShapeDtypeStruct((nb * seq, ws), F32)] * 2
    out_specs = [pl.BlockSpec((tm, wa), lambda b, t: (t, b))] + [tspec(wr)] * 4 + [tspec(ws)] * 2
    if n_prev < 0:
        out_shape += [jax.ShapeDtypeStruct((nb * seq, ws), F32)]
        out_specs += [tspec(ws)]
    else:
        stack = lambda n: pl.BlockSpec((n, ws, tm), lambda b, t: (0, b, t))
        if n_prev > 0:
            args += list(kv_prev)
            in_specs += [stack(n_prev)] * 2
        out_shape += [jax.ShapeDtypeStruct((n_prev + 1, nb * ws, seq), F32)] * 2
        out_specs += [stack(n_prev + 1)] * 2
    return pl.pallas_call(
        functools.partial(_proj_in_kernel, wa=wa, wr=wr, ws=ws, n_prev=n_prev),
        grid=(nb, nt),
        in_specs=in_specs,
        out_specs=out_specs,
        out_shape=out_shape,
        compiler_params=_params(("arbitrary", "arbitrary")),
        name="proj_in",
    )(*args)


def _s5_kernel(u_ref, s0re_ref, s0im_ref, are_ref, aim_ref, bblk_ref, cblk_ref, dsk_ref,
               gw_ref, gb_ref, y_ref, fre_ref, fim_ref, drv_scr, hre_scr, him_scr, *, nb, tc):
    ns = hre_scr.shape[1]

    @pl.when(pl.program_id(0) == 0)
    def _():
        hre_scr[...] = s0re_ref[...]
        him_scr[...] = s0im_ref[...]

    u = u_ref[...]
    drv_scr[...] = _dot(u, bblk_ref[...])
    are = jnp.broadcast_to(are_ref[...], (nb, ns))
    aim = jnp.broadcast_to(aim_ref[...], (nb, ns))

    def step(t, carry):
        hre, him = carry
        rows = pl.ds(pl.multiple_of(t * nb, nb), nb)
        nre = are * hre - aim * him + drv_scr[rows, 0:ns]
        nim = are * him + aim * hre + drv_scr[rows, ns:2 * ns]
        drv_scr[rows, 0:ns] = nre
        drv_scr[rows, ns:2 * ns] = nim
        return nre, nim

    hre, him = lax.fori_loop(0, tc, step, (hre_scr[...], him_scr[...]))
    hre_scr[...] = hre
    him_scr[...] = him
    fre_ref[...] = hre
    fim_ref[...] = him
    y = _dot(drv_scr[...], cblk_ref[...]) + dsk_ref[...] * u
    gy = jax.nn.gelu(y)
    y_ref[...] = gy * jax.nn.sigmoid(_dot(gy, gw_ref[...]) + gb_ref[...])


def _s5(u_tm, s0re, s0im, are, aim, bblk, cblk, dsk, gw, gb, *, nb, seq, tc):
    wa = u_tm.shape[1]
    ns = are.shape[1]
    rows = tc * nb
    return pl.pallas_call(
        functools.partial(_s5_kernel, nb=nb, tc=tc),
        grid=(seq // tc,),
        in_specs=[pl.BlockSpec((rows, wa), lambda i: (i, 0)),
                  _const_spec((nb, ns)), _const_spec((nb, ns)),
                  _const_spec((1, ns)), _const_spec((1, ns)),
                  _const_spec(bblk.shape), _const_spec(cblk.shape), _const_spec((1, wa)),
                  _const_spec(gw.shape), _const_spec((1, wa))],
        out_specs=[pl.BlockSpec((rows, wa), lambda i: (i, 0)),
                   pl.BlockSpec((nb, ns), lambda i: (0, 0)),
                   pl.BlockSpec((nb, ns), lambda i: (0, 0))],
        out_shape=[jax.ShapeDtypeStruct((seq * nb, wa), F32),
                   jax.ShapeDtypeStruct((nb, ns), F32),
                   jax.ShapeDtypeStruct((nb, ns), F32)],
        scratch_shapes=[pltpu.VMEM((rows, 2 * ns), F32), pltpu.VMEM((nb, ns), F32),
                        pltpu.VMEM((nb, ns), F32)],
        compiler_params=_params(("arbitrary",)),
        name="s5",
    )(u_tm, s0re, s0im, are, aim, bblk, cblk, dsk, gw, gb)


def _head_lane_mask(width, h):
    lane = lax.broadcasted_iota(jnp.int32, (1, width), 1)
    return (lane >> 6) == h


def _ret_intra(q, k, v, dmask_ref, nh):
    wr = q.shape[1]
    o = jnp.zeros(q.shape, F32)
    for h in range(nh):
        mh = _head_lane_mask(wr, h)
        s = _dot_nt(jnp.where(mh, q, 0.0), k) * dmask_ref[h]
        o = jnp.where(mh, _dot(s, v), o)
    return o


def _ret_finish(o, gate, gn_ref, gm_ref):
    gm = gm_ref[...]
    xc = o - _dot_split(o, gm)
    var = _dot_split(xc * xc, gm)
    return jax.nn.silu(gate) * (xc * lax.rsqrt(var + EPS) * gn_ref[...])


def _ret_kernel(q_ref, k_ref, v_ref, g_ref, dmask_ref, xi_ref, zeta_ref, dcol_ref, gn_ref, gm_ref,
                y_ref, sfin_ref, s_scr, *, nh):
    nseq, _, wr = q_ref.shape

    @pl.when(pl.program_id(1) == 0)
    def _():
        s_scr[...] = jnp.zeros(s_scr.shape, F32)

    row_h = lax.broadcasted_iota(jnp.int32, (wr, wr), 0) >> 6
    col_h = lax.broadcasted_iota(jnp.int32, (wr, wr), 1) >> 6
    for i in range(nseq):
        q = q_ref[i]
        k = k_ref[i]
        v = v_ref[i]
        s_prev = s_scr[i]
        o = _ret_intra(q, k, v, dmask_ref, nh) + _dot(q * xi_ref[...], s_prev)
        kv = _dot((k * zeta_ref[...]).T, v)
        s_new = s_prev * dcol_ref[...] + jnp.where(row_h == col_h, kv, 0.0)
        s_scr[i] = s_new
        sfin_ref[i] = s_new
        y_ref[i] = _ret_finish(o, g_ref[i], gn_ref, gm_ref)


def _retention(q, k, v, g, dmask, xi, zeta, dcol, gn, gm, *, nb, seq, chunk, nh, nseq):
    wr = q.shape[-1]
    nc = seq // chunk
    tspec = pl.BlockSpec((nseq, chunk, wr), lambda b, c: (b, c, 0))
    return pl.pallas_call(
        functools.partial(_ret_kernel, nh=nh),
        grid=(nb // nseq, nc),
        in_specs=[tspec] * 4 + [_const_spec(dmask.shape), _const_spec(xi.shape),
                                _const_spec(zeta.shape), _const_spec((1, wr)),
                                _const_spec((1, wr)), _const_spec((wr, wr))],
        out_specs=[tspec, pl.BlockSpec((nseq, wr, wr), lambda b, c: (b, 0, 0))],
        out_shape=[jax.ShapeDtypeStruct((nb, seq, wr), F32),
                   jax.ShapeDtypeStruct((nb, wr, wr), F32)],
        scratch_shapes=[pltpu.VMEM((nseq, wr, wr), F32)],
        compiler_params=_params(("arbitrary", "arbitrary")),
        name="retention",
    )(q, k, v, g, dmask, xi, zeta, dcol, gn, gm)


def _ret_step_kernel(q_ref, k_ref, v_ref, g_ref, dmask_ref, xi_ref, zeta_ref, dcol_ref, gn_ref,
                     gm_ref, s0_ref, exp_ref, y_ref, snew_ref, *, nh, ntok):
    t_rows, wr = q_ref.shape
    sw = s0_ref.shape[0]
    q = q_ref[...]
    k = k_ref[...]
    v = v_ref[...]
    s0 = s0_ref[...]
    o = _ret_intra(q, k, v, dmask_ref, nh)
    qx = q * xi_ref[...]
    kz = k * zeta_ref[...]
    own_b = ((lax.broadcasted_iota(jnp.int32, (t_rows, sw), 1) >> 6)
             == lax.broadcasted_iota(jnp.int32, (t_rows, sw), 0) // ntok)
    s_new = s0 * dcol_ref[...]
    for h in range(nh):
        mh = _head_lane_mask(wr, h)
        q_exp = jnp.where(own_b, _dot(jnp.where(mh, qx, 0.0), exp_ref[h]), 0.0)
        k_exp = jnp.where(own_b, _dot(jnp.where(mh, kz, 0.0), exp_ref[h]), 0.0)
        o = o + jnp.where(mh, _dot(q_exp, s0), 0.0)
        s_new = s_new + jnp.where(mh, _dot(k_exp.T, v), 0.0)
    snew_ref[...] = s_new
    y_ref[...] = _ret_finish(o, g_ref[...], gn_ref, gm_ref)


def _retention_step(q, k, v, g, dmask, xi, zeta, dcol, gn, gm, s0cat, expand, *, nh, ntok):
    t_rows, wr = q.shape
    args = (q, k, v, g, dmask, xi, zeta, dcol, gn, gm, s0cat, expand)
    return pl.pallas_call(
        functools.partial(_ret_step_kernel, nh=nh, ntok=ntok),
        grid=(1,),
        in_specs=[_const_spec(a.shape) for a in args],
        out_specs=[_const_spec((t_rows, wr), False), _const_spec(s0cat.shape, False)],
        out_shape=[jax.ShapeDtypeStruct((t_rows, wr), F32),
                   jax.ShapeDtypeStruct(s0cat.shape, F32)],
        compiler_params=_params(("arbitrary",)),
        name="retention_step",
    )(*args)


def _sb_fail(z2, causal):
    g = jnp.maximum(z2, 0.0) + jnp.log(1.0 + jnp.exp2(-jnp.abs(z2))) * LOG2E
    return g if causal is None else jnp.where(causal, g, 0.0)


def _sb_weight(z2, suffix, run, causal):
    w = jnp.exp2(z2 - suffix - run)
    return w if causal is None else jnp.where(causal, w, 0.0)


def _sb_kernel(ext_ref, q_ref, k_ref, v_ref, u_ref, o_ref, acc_scr, run_scr, *, tq):
    qi = pl.program_id(2)
    q = q_ref[...]
    width = q.shape[1]
    nh = width // HEAD_DIM
    rows = nh * tq
    umat = u_ref[...]
    q2 = jnp.concatenate([jnp.where(_head_lane_mask(width, h), q, 0.0) for h in range(nh)],
                         axis=0).astype(BF16)
    q2 = jnp.concatenate([q2, ext_ref[...]], axis=1)
    ones = (lax.broadcasted_iota(jnp.int32, (LANES, tq), 0) < BIAS_PIECES).astype(BF16)
    causal = (lax.broadcasted_iota(jnp.int32, (rows, tq), 1)
              < (lax.broadcasted_iota(jnp.int32, (rows, tq), 0) & (tq - 1)))

    def tiles(j, count, diagonal):
        acc = acc_scr[...]
        run = run_scr[...]
        masks = [causal if diagonal else None] + [None] * (count - 1)
        keys = [pl.ds(pl.multiple_of((j - i) * tq, tq), tq) for i in range(count)]
        zs = [jnp.dot(q2, jnp.concatenate([k_ref[:, ks].astype(BF16), ones], axis=0),
                      preferred_element_type=F32) for ks in keys]
        stacked = _dot(jnp.concatenate([_sb_fail(z, m) for z, m in zip(zs, masks)], axis=0), umat)
        suffixes = [stacked[i * rows:(i + 1) * rows] for i in range(count)]
        for z, suffix, ks, m in zip(zs, suffixes, keys, masks):
            acc = acc + _dot(_sb_weight(z, suffix, run, m), v_ref[ks, :])
            run = run + suffix[:, 0:1]
        acc_scr[...] = acc
        run_scr[...] = run

    acc_scr[...] = jnp.zeros(acc_scr.shape, F32)
    run_scr[...] = jnp.zeros(run_scr.shape, F32)
    ntile = qi + 1
    top = qi
    done = jnp.int32(0)
    for bit, count in ((0, 1), (1, 2), (2, 4)):
        has = (ntile >> bit) & 1
        for diagonal in (True, False):
            @pl.when((has == 1) & ((done == 0) == diagonal))
            def _(top=top, count=count, diagonal=diagonal):
                tiles(top, count, diagonal)

        top = top - has * count
        done = done + has

    @pl.when((ntile >= 8) & (done == 0))
    def _():
        tiles(top, 8, True)

    first8 = jnp.where(done == 0, 1, 0)

    @pl.loop(first8, ntile >> 3)
    def _(n):
        tiles(top - 8 * n, 8, False)

    acc = acc_scr[...]
    out = acc[0:tq]
    for h in range(1, nh):
        out = jnp.where(_head_lane_mask(width, h), acc[h * tq:(h + 1) * tq], out)
    o_ref[...] = out


def _stick_breaking(q, kt, v, ext, umat, *, layer, nb, seq, tq):
    ws = q.shape[1]
    nq = seq // tq
    nhp = ws // LANES
    qspec = pl.BlockSpec((tq, LANES), lambda b, hp, i: (b * nq + i, hp))
    kspec = pl.BlockSpec((None, LANES, seq), lambda b, hp, i: (layer, b * nhp + hp, 0))
    return pl.pallas_call(
        functools.partial(_sb_kernel, tq=tq),
        grid=(nb, nhp, nq),
        in_specs=[pl.BlockSpec((None,) + ext.shape[1:], lambda b, hp, i: (hp, 0, 0)), qspec, kspec,
                  pl.BlockSpec((seq, LANES), lambda b, hp, i: (b, hp)), _const_spec((tq, tq))],
        out_specs=qspec,
        out_shape=jax.ShapeDtypeStruct((nb * seq, ws), F32),
        scratch_shapes=[pltpu.VMEM((ext.shape[1], LANES), F32), pltpu.VMEM((ext.shape[1], 1), F32)],
        compiler_params=_params(("arbitrary", "arbitrary", "arbitrary")),
        name="stick_breaking",
    )(ext, q, kt, v, umat)


def _bias_pieces(bias, tq):
    left = bias.astype(F32) * LOG2E
    parts = []
    for _ in range(BIAS_PIECES):
        part = left.astype(BF16)
        parts.append(part)
        left = left - part.astype(F32)
    pieces = jnp.stack(parts, axis=1)
    pieces = jnp.pad(pieces, ((0, 0), (0, LANES - BIAS_PIECES)))
    nh = LANES // HEAD_DIM
    return jnp.repeat(pieces, tq, axis=0).reshape(bias.shape[0] // nh, nh * tq, LANES)


def _sb_step_kernel(pt_ref, q_ref, kn_ref, vn_ref, bias_ref, u_ref, *rest, pp, nh):
    k_pages = rest[:pp]
    v_pages = rest[pp:2 * pp]
    o_ref = rest[2 * pp]
    acc_scr, run_scr = rest[2 * pp + 1:]
    j = pl.program_id(1)
    q = q_ref[...].astype(BF16)
    rows, width = q.shape
    bias = bias_ref[...] * LOG2E
    umat = u_ref[...]

    def sweep(z_pages, causal):
        n = len(z_pages)
        z = jnp.concatenate(z_pages, axis=0) + jnp.concatenate([bias] * n, axis=0)
        suffix = _dot(_sb_fail(z, causal), umat)
        run = run_scr[...]
        later = [None] * n
        for p in reversed(range(n)):
            later[p] = run
            run = run + suffix[p * rows:(p + 1) * rows, 0:1]
        w = _sb_weight(z, suffix, jnp.concatenate(later, axis=0), causal)
        run_scr[...] = run
        return [w[p * rows:(p + 1) * rows] for p in range(n)]

    @pl.when(j == 0)
    def _():
        run_scr[...] = jnp.zeros(run_scr.shape, F32)
        pad = jnp.zeros((PAGE_SIZE - kn_ref.shape[0], width), F32)
        kn = jnp.concatenate([kn_ref[...], pad], axis=0)
        vn = jnp.concatenate([vn_ref[...], pad], axis=0)
        row_t = lax.broadcasted_iota(jnp.int32, (rows, PAGE_SIZE), 0) // nh
        col = lax.broadcasted_iota(jnp.int32, (rows, PAGE_SIZE), 1)
        (w,) = sweep([_dot_nt(q, kn)], col < row_t)
        acc_scr[...] = _dot(w, vn)

    @pl.when(j > 0)
    def _():
        kt = jnp.concatenate([k[...].astype(BF16) for k in k_pages], axis=1)
        z = jnp.dot(q, kt, preferred_element_type=F32)
        w = sweep([z[:, p * PAGE_SIZE:(p + 1) * PAGE_SIZE] for p in range(pp)], None)
        vt = jnp.concatenate([v[...].astype(BF16) for v in v_pages], axis=1)
        acc_scr[...] += _dot_nt(jnp.concatenate(w, axis=1), vt)

    @pl.when(j == pl.num_programs(1) - 1)
    def _():
        lane_h = lax.broadcasted_iota(jnp.int32, (rows, width), 1) >> 6
        row_h = lax.broadcasted_iota(jnp.int32, (rows, width), 0) % nh
        own = jnp.where(lane_h == row_h, acc_scr[...], 0.0)
        o_ref[...] = jnp.sum(own.reshape(rows // nh, nh, width), axis=1)


def _stick_breaking_step(page_table, q_exp, k_new, v_new, bias_rows, umat, cache_k, cache_v, *,
                         layer, pp, nh):
    nb, rows, width = q_exp.shape
    n_pages = page_table.shape[1]
    n_steps = n_pages // pp
    ntok = rows // nh

    def page_map(slot):
        def index(b, j, pt):
            return (layer, pt[b, n_pages - jnp.maximum(j, 1) * pp + slot], 0, 0)
        return index

    page_specs = [pl.BlockSpec((None, None, width, PAGE_SIZE), page_map(s)) for s in range(pp)]
    per_b = lambda r: pl.BlockSpec((None, r, width), lambda b, j, pt: (b, 0, 0))
    new_spec = per_b(k_new.shape[1])
    grid_spec = pltpu.PrefetchScalarGridSpec(
        num_scalar_prefetch=1,
        grid=(nb, n_steps + 1),
        in_specs=[per_b(rows), new_spec, new_spec,
                  pl.BlockSpec((rows, 1), lambda b, j, pt: (0, 0)),
                  pl.BlockSpec((PAGE_SIZE, PAGE_SIZE), lambda b, j, pt: (0, 0))]
        + page_specs + page_specs,
        out_specs=per_b(ntok),
        scratch_shapes=[pltpu.VMEM((rows, width), F32), pltpu.VMEM((rows, 1), F32)],
    )
    return pl.pallas_call(
        functools.partial(_sb_step_kernel, pp=pp, nh=nh),
        grid_spec=grid_spec,
        out_shape=jax.ShapeDtypeStruct((nb, ntok, width), F32),
        compiler_params=_params(("arbitrary", "arbitrary")),
        name="stick_breaking_step",
    )(page_table, q_exp, k_new, v_new, bias_rows, umat, *([cache_k] * pp), *([cache_v] * pp))


def _mix_ffn_kernel(*refs, tm, wa, wr, decode_tokens):
    if decode_tokens:
        (h_ref, ya_ref, yr_ref, ys_ref, p_ref, mg_ref, wout_ref, nf_ref, wup_ref, wgate_ref,
         cw_ref, cb_ref, wdown_ref, pn_ref, pgate_ref, pproj_ref, prev1_ref, prev2_ref,
         out_ref, conv_ref, ext_scr, acc_scr) = refs
    else:
        (h_ref, ya_ref, yr_ref, ys_ref, p_ref, mg_ref, wout_ref, nf_ref, wup_ref, wgate_ref,
         cw_ref, cb_ref, wdown_ref, pn_ref, pgate_ref, pproj_ref,
         out_ref, conv_ref, ext_scr, acc_scr, carry_scr) = refs

        @pl.when(pl.program_id(1) == 0)
        def _():
            carry_scr[...] = jnp.zeros(carry_scr.shape, F32)

    mg = mg_ref[...]
    a1 = wa + wr
    merged = (_dot(_rms(ya_ref[...], mg[:, 0:wa]), wout_ref[0:wa, :])
              + _dot(_rms(yr_ref[...], mg[:, wa:a1]), wout_ref[wa:a1, :])
              + _dot(_rms(ys_ref[...], mg[:, a1:]), wout_ref[a1:, :]))
    h_mid = h_ref[...] + merged
    n2 = _rms(h_mid, nf_ref[...]).astype(BF16)
    acc_scr[...] = jnp.zeros(acc_scr.shape, F32)
    if decode_tokens:
        ext_scr[0:SUBLANES, :] = jnp.zeros((SUBLANES, FF_CHUNK), F32)
        tok = lax.broadcasted_iota(jnp.int32, (tm, FF_CHUNK), 0) % decode_tokens

    acts = []
    nchunk = wup_ref.shape[1] // FF_CHUNK
    ngroup = -(-nchunk // DOWN_GROUP)
    group_ends = {(i + 1) * nchunk // ngroup for i in range(ngroup)}
    for lo in range(0, wup_ref.shape[1], FF_CHUNK):
        cols = slice(lo, lo + FF_CHUNK)
        u = jnp.dot(n2, wup_ref[:, cols], preferred_element_type=F32)
        g = jnp.dot(n2, wgate_ref[:, cols], preferred_element_type=F32)
        if not decode_tokens:
            ext_scr[0:SUBLANES, :] = carry_scr[:, cols]
        ext_scr[SUBLANES:SUBLANES + tm, :] = u
        prev1 = ext_scr[pl.ds(SUBLANES - 1, tm), :]
        prev2 = ext_scr[pl.ds(SUBLANES - 2, tm), :]
        if decode_tokens:
            prev1 = jnp.where(tok >= 1, prev1, prev1_ref[:, cols])
            prev2 = jnp.where(tok >= 2, prev2, prev2_ref[:, cols])
            conv_ref[:, cols] = u
        else:
            last = u[tm - SUBLANES:tm, :]
            carry_scr[:, cols] = last
            conv_ref[:, cols] = last
        conv = cb_ref[:, cols] + cw_ref[0:1, cols] * prev2
        conv = conv + cw_ref[1:2, cols] * prev1
        conv = conv + cw_ref[2:3, cols] * u
        acts.append((jax.nn.gelu(conv) * g).astype(BF16))
        if (lo + FF_CHUNK) // FF_CHUNK in group_ends:
            width = len(acts) * FF_CHUNK
            acc_scr[...] += jnp.dot(jnp.concatenate(acts, axis=1),
                                    wdown_ref[lo + FF_CHUNK - width:lo + FF_CHUNK, :],
                                    preferred_element_type=F32)
            acts = []
    h2 = h_mid + acc_scr[...]
    gate = jax.nn.sigmoid(_dot(_rms(h2, pn_ref[...]), pgate_ref[...]))
    out_ref[...] = h2 + _dot(p_ref[...], pproj_ref[...]) * gate


def _mix_ffn(h2, ya, yr, ys, p2, w, prev=None, *, nb, seq, tm, decode_tokens=0):
    d = h2.shape[1]
    wa, wr, ws = w["d_skip"].shape[1], yr.shape[1], ys.shape[1]
    pd = p2.shape[1]
    nt = seq // tm
    dff = w["w_up"].shape[1]
    row = lambda b, t: (b * nt + t, 0)
    tspec = lambda width: pl.BlockSpec((tm, width), row)
    ya_spec = tspec(wa) if decode_tokens else pl.BlockSpec((tm, wa), lambda b, t: (t, b))
    weights = [w["merge"], w["w_out"], w["norm_ffn"], w["w_up"], w["w_gate"], w["conv_w"],
               w["conv_b"], w["w_down"], w["ple_norm"], w["ple_gate"], w["ple_proj"]]
    in_specs = [tspec(d), ya_spec, tspec(wr), tspec(ws), tspec(pd)]
    in_specs += [_const_spec(a.shape) for a in weights]
    args = [h2, ya, yr, ys, p2] + weights
    scratch = [pltpu.VMEM((tm + SUBLANES, FF_CHUNK), F32), pltpu.VMEM((tm, d), F32)]
    if decode_tokens:
        in_specs += [_const_spec(prev[0].shape), _const_spec(prev[1].shape)]
        args += list(prev)
        conv_rows = tm
    else:
        scratch.append(pltpu.VMEM((SUBLANES, dff), F32))
        conv_rows = SUBLANES
    return pl.pallas_call(
        functools.partial(_mix_ffn_kernel, tm=tm, wa=wa, wr=wr, decode_tokens=decode_tokens),
        grid=(nb, nt),
        in_specs=in_specs,
        out_specs=[tspec(d),
                   pl.BlockSpec((None, conv_rows, dff), lambda b, t: (b, 0, 0))],
        out_shape=[jax.ShapeDtypeStruct((nb * seq, d), F32),
                   jax.ShapeDtypeStruct((nb, conv_rows, dff), F32)],
        scratch_shapes=scratch,
        compiler_params=_params(("arbitrary", "arbitrary")),
        name="mix_ffn_step" if decode_tokens else "mix_ffn",
    )(*args)


def _block_diag(blocks):
    g, r, c = blocks.shape
    eye = jnp.eye(g, dtype=blocks.dtype)
    return (eye[:, None, :, None] * blocks[:, :, None, :]).reshape(g * r, g * c)


def _s5_params(lam_re, lam_im, log_dt, b_re, b_im, c_re, c_im):
    dt = jnp.exp(log_dt.astype(F32))[:, None]
    lr = lam_re.astype(F32)
    li = lam_im.astype(F32)
    mag = jnp.exp(lr * dt)
    ang = li * dt
    ab_re = mag * jnp.cos(ang)
    ab_im = mag * jnp.sin(ang)
    den = lr * lr + li * li
    f_re = ((ab_re - 1.0) * lr + ab_im * li) / den
    f_im = (ab_im * lr - (ab_re - 1.0) * li) / den
    bb_re = f_re[..., None] * b_re - f_im[..., None] * b_im
    bb_im = f_re[..., None] * b_im + f_im[..., None] * b_re
    to_in = lambda m: _block_diag(jnp.swapaxes(m, 1, 2))
    bblk = jnp.concatenate([to_in(bb_re), to_in(bb_im)], axis=1)
    to_out = lambda m: _block_diag(jnp.swapaxes(m, 1, 2))
    cblk = jnp.concatenate([to_out(c_re), -to_out(c_im)], axis=0)
    return ab_re.reshape(1, -1), ab_im.reshape(1, -1), bblk.astype(BF16), cblk.astype(BF16)


def _ret_consts(nh, chunk, group):
    log_g = jnp.log1p(-jnp.exp2(-5.0 - jnp.arange(nh, dtype=F32)))
    idx = jnp.arange(chunk, dtype=jnp.int32)
    pos = (idx % group).astype(F32)
    diff = pos[:, None] - pos[None, :]
    ok = (diff >= 0) & ((idx // group)[:, None] == (idx // group)[None, :])
    dmask = jnp.where(ok[None], jnp.exp(jnp.where(ok, diff, 0.0)[None] * log_g[:, None, None]), 0.0)
    xi = jnp.exp((pos + 1.0)[None, :] * log_g[:, None])
    zeta = jnp.exp((group - 1.0 - pos)[None, :] * log_g[:, None])
    d_chunk = jnp.exp(group * log_g)
    lanes = lambda t: jnp.repeat(t.T, HEAD_DIM, axis=1)
    return dmask, lanes(xi), lanes(zeta), jnp.repeat(d_chunk, HEAD_DIM)[None, :]


def _rope_tables(pos, nh):
    half = HEAD_DIM // 2
    inv = jnp.power(ROPE_BASE, -jnp.arange(half, dtype=F32) / half)
    ang = pos.astype(F32)[:, None] * inv[None, :]
    cos = jnp.cos(ang)
    sin = jnp.sin(ang)
    return (jnp.tile(jnp.concatenate([cos, cos], axis=1), (1, nh)),
            jnp.tile(jnp.concatenate([-sin, sin], axis=1), (1, nh)))


def _group_mean_matrix(width):
    g = jnp.arange(width) // HEAD_DIM
    return jnp.where(g[:, None] == g[None, :], 1.0 / HEAD_DIM, 0.0).astype(BF16)


def _suffix_matrix(n):
    i = jnp.arange(n)
    return (i[:, None] >= i[None, :]).astype(BF16)


def _layer_weights(i, wts):
    (norm_mix_gain, w_in, lam_re, lam_im, log_dt, b_re, b_im, c_re, c_im, s5_d, glu_w, glu_b,
     ret_gn, q_gain, k_gain, sb_bias, merge, w_out, norm_ffn, w_up, w_gate, conv_w, conv_b, w_down,
     ple_norm, ple_gate, ple_proj) = [t[i] for t in wts]
    nh_s = sb_bias.shape[0]
    row = lambda v: v.reshape(1, -1)
    are, aim, bblk, cblk = _s5_params(lam_re, lam_im, log_dt, b_re, b_im, c_re, c_im)
    conv_w8 = jnp.concatenate([conv_w, jnp.zeros((SUBLANES - CONV_W, conv_w.shape[1]), F32)], axis=0)
    return dict(
        norm_mix=row(norm_mix_gain), w_in=w_in.astype(BF16),
        are=are, aim=aim, bblk=bblk, cblk=cblk, d_skip=row(s5_d), glu_w=glu_w.astype(BF16),
        glu_b=row(glu_b), ret_gn=row(ret_gn),
        q_gain=row(jnp.tile(q_gain, nh_s)), k_gain=row(jnp.tile(k_gain, nh_s)), sb_bias=sb_bias,
        merge=row(merge), w_out=w_out.astype(BF16), norm_ffn=row(norm_ffn),
        w_up=w_up.astype(BF16), w_gate=w_gate.astype(BF16), conv_w=conv_w8, conv_b=row(conv_b),
        w_down=w_down.astype(BF16),
        ple_norm=row(ple_norm), ple_gate=ple_gate.astype(BF16), ple_proj=ple_proj.astype(BF16))


def _prompt_layer(h2, p2, w, consts, kv_prev, *, layer, nb, seq):
    wa, wr, ws, nh_r = consts["wa"], consts["wr"], consts["ws"], consts["nh_r"]
    tm, tq, chunk, tc = consts["tm"], consts["tq"], consts["chunk"], consts["tc"]
    ua, qr, kr, vr, gr, qs, vs, kst, vst = _proj_in(
        h2, w["norm_mix"], w["w_in"], consts["cos"], consts["sin"], w["q_gain"], w["k_gain"],
        consts["gm_s"], kv_prev, nb=nb, seq=seq, tm=tm, wa=wa, wr=wr, ws=ws, n_prev=layer)
    zeros = jnp.zeros((nb, w["are"].shape[1]), F32)
    ya, sre, sim = _s5(ua.reshape(seq * nb, wa), zeros, zeros, w["are"], w["aim"], w["bblk"],
                       w["cblk"], w["d_skip"], w["glu_w"], w["glu_b"], nb=nb, seq=seq, tc=tc)
    seq3 = lambda a: a.reshape(nb, seq, wr)
    yr, sret = _retention(seq3(qr), seq3(kr), seq3(vr), seq3(gr), consts["dmask"], consts["xi"],
                          consts["zeta"], consts["dcol"], w["ret_gn"], consts["gm_r"], nb=nb,
                          seq=seq, chunk=chunk, nh=nh_r, nseq=consts["ret_nseq"])
    yr = yr.reshape(nb * seq, wr)
    ys = _stick_breaking(qs, kst, vs, _bias_pieces(w["sb_bias"], tq), consts["later_q"],
                         layer=layer, nb=nb, seq=seq, tq=tq)
    h_out, conv = _mix_ffn(h2, ya.reshape(seq, nb * wa), yr, ys, p2, w, nb=nb, seq=seq, tm=tm)
    ret_state = jnp.stack([sret[:, h * HEAD_DIM:(h + 1) * HEAD_DIM, h * HEAD_DIM:(h + 1) * HEAD_DIM]
                           for h in range(nh_r)], axis=1)
    conv_state = conv[:, SUBLANES - (CONV_W - 1):, :]
    return h_out, (kst, vst), (sre, sim, ret_state, conv_state)


def _decode_layer(h2, p2, w, consts, past, *, layer, nb, ntok):
    wa, wr, ws, nh_r, nh_s = consts["wa"], consts["wr"], consts["ws"], consts["nh_r"], consts["nh_s"]
    rows = nb * ntok
    ua, qr, kr, vr, gr, qs, ks, vs = _proj_in(
        h2, w["norm_mix"], w["w_in"], consts["cos"], consts["sin"], w["q_gain"], w["k_gain"],
        consts["gm_s"], nb=1, seq=rows, tm=rows, wa=wa, wr=wr, ws=ws)
    to_tm = lambda a: a.reshape(nb, ntok, -1).transpose(1, 0, 2).reshape(rows, -1)
    from_tm = lambda a: a.reshape(ntok, nb, -1).transpose(1, 0, 2).reshape(rows, -1)
    ya, sre, sim = _s5(to_tm(ua), past["ssm_re"], past["ssm_im"], w["are"], w["aim"], w["bblk"],
                       w["cblk"], w["d_skip"], w["glu_w"], w["glu_b"], nb=nb, seq=ntok, tc=ntok)
    s0cat = past["ret"].transpose(0, 2, 1, 3).reshape(nb * HEAD_DIM, wr)
    yr, snew = _retention_step(qr, kr, vr, gr, consts["dmask"], consts["xi"], consts["zeta"],
                               consts["dcol"], w["ret_gn"], consts["gm_r"], s0cat,
                               consts["expand"], nh=nh_r, ntok=ntok)
    ret_state = snew.reshape(nb, HEAD_DIM, nh_r, HEAD_DIM).transpose(0, 2, 1, 3)
    q_rep = jnp.repeat(qs.reshape(nb, ntok, ws), nh_s, axis=1)
    q_exp = jnp.where(consts["q_own"], q_rep, 0.0)
    pad = lambda a: jnp.pad(a.reshape(nb, ntok, ws), ((0, 0), (0, -ntok % SUBLANES), (0, 0)))
    bias_rows = jnp.tile(w["sb_bias"], ntok).reshape(ntok * nh_s, 1)
    ys = _stick_breaking_step(past["page_table"], q_exp, pad(ks), pad(vs), bias_rows,
                              consts["later_p"], past["cache_k"], past["cache_v"], layer=layer,
                              pp=consts["pp"], nh=nh_s).reshape(rows, ws)
    buf = past["conv"]
    zero = jnp.zeros_like(buf[:, :1])
    prev1 = jnp.concatenate([buf[:, 1:2]] + [zero] * (ntok - 1), axis=1).reshape(rows, -1)
    prev2 = jnp.concatenate([buf[:, 0:1], buf[:, 1:2]] + [zero] * (ntok - 2), axis=1).reshape(rows, -1)
    h_out, conv = _mix_ffn(h2, from_tm(ya), yr, ys, p2, w, (prev1, prev2),
                           nb=1, seq=rows, tm=rows, decode_tokens=ntok)
    conv_state = conv.reshape(nb, ntok, -1)[:, ntok - (CONV_W - 1):, :]
    return h_out, (ks, vs, sre, sim, ret_state, conv_state)


def kernel(x_prompt, x_sample, cache_k, cache_v, state_ssm_re, state_ssm_im, state_ret, state_conv, page_table, p_prompt, p_sample, norm_mix_gain, w_in, s5_lambda_re, s5_lambda_im, s5_log_dt, s5_b_re, s5_b_im, s5_c_re, s5_c_im, s5_d, s5_glu_w, s5_glu_b, ret_gn_gain, sb_q_gain, sb_k_gain, sb_logit_bias, merge_gain, w_out, norm_ffn_gain, ffn_w_up, ffn_w_gate, ffn_conv_w, ffn_conv_b, ffn_w_down, ple_norm_gain, ple_w_gate, ple_w_proj):
    wts = (norm_mix_gain, w_in, s5_lambda_re, s5_lambda_im, s5_log_dt, s5_b_re, s5_b_im, s5_c_re,
           s5_c_im, s5_d, s5_glu_w, s5_glu_b, ret_gn_gain, sb_q_gain, sb_k_gain, sb_logit_bias,
           merge_gain, w_out, norm_ffn_gain, ffn_w_up, ffn_w_gate, ffn_conv_w, ffn_conv_b,
           ffn_w_down, ple_norm_gain, ple_w_gate, ple_w_proj)
    depth = w_in.shape[0]
    nb, seq, d = x_prompt.shape
    nbs, ntok, _ = x_sample.shape
    past_len = page_table.shape[1] * PAGE_SIZE
    wa = s5_d.shape[1]
    wr = ret_gn_gain.shape[1]
    nh_r = wr // HEAD_DIM
    nh_s = sb_logit_bias.shape[1]
    ws = nh_s * HEAD_DIM
    n_pool = cache_k.shape[1]
    rows_s = nbs * ntok

    shared = dict(wa=wa, wr=wr, ws=ws, nh_r=nh_r, nh_s=nh_s, gm_s=_group_mean_matrix(ws),
                  gm_r=_group_mean_matrix(wr))
    chunk = min(128, seq)
    cos_p, sin_p = _rope_tables(jnp.arange(seq, dtype=jnp.int32), nh_r)
    dmask_p, xi_p, zeta_p, dcol_p = _ret_consts(nh_r, chunk, chunk)
    tq = min(256, seq)
    consts_p = dict(shared, cos=cos_p, sin=sin_p, dmask=dmask_p, xi=xi_p, zeta=zeta_p, dcol=dcol_p,
                    tm=min(512, seq), tq=tq, chunk=chunk, tc=min(128, seq),
                    later_q=_suffix_matrix(tq), ret_nseq=RET_SEQS if nb % RET_SEQS == 0 else 1)
    pos_s = past_len + jnp.tile(jnp.arange(ntok, dtype=jnp.int32), nbs)
    cos_s, sin_s = _rope_tables(pos_s, nh_r)
    dmask_s, xi_s, zeta_s, dcol_s = _ret_consts(nh_r, rows_s, ntok)
    lane_src = jnp.arange(wr)
    lane_dst = jnp.arange(nbs * HEAD_DIM)
    expand = jnp.stack([(lane_src[:, None] == h * HEAD_DIM + lane_dst[None, :] % HEAD_DIM)
                        for h in range(nh_r)]).astype(BF16)
    q_own = (jnp.arange(ws)[None, :] // HEAD_DIM) == (jnp.arange(ntok * nh_s)[:, None] % nh_s)
    consts_s = dict(shared, cos=cos_s, sin=sin_s, dmask=dmask_s, xi=xi_s, zeta=zeta_s, dcol=dcol_s,
                    expand=expand, q_own=q_own[None], later_p=_suffix_matrix(PAGE_SIZE),
                    pp=min(PAGES_PER_STEP, page_table.shape[1]))

    ck = cache_k.transpose(0, 1, 3, 4, 2).reshape(depth, n_pool, ws, PAGE_SIZE)
    cv = cache_v.transpose(0, 1, 3, 4, 2).reshape(depth, n_pool, ws, PAGE_SIZE)
    h_p = x_prompt.reshape(nb * seq, d)
    h_s = x_sample.reshape(rows_s, d)
    outs_p, outs_s = [], []
    kv_p = None
    for i in range(depth):
        w = _layer_weights(i, wts)
        h_p, kv_p, st_p = _prompt_layer(h_p, p_prompt[i].reshape(nb * seq, -1), w, consts_p, kv_p,
                                        layer=i, nb=nb, seq=seq)
        past = dict(page_table=page_table, cache_k=ck, cache_v=cv,
                    ssm_re=state_ssm_re[i].reshape(nbs, -1), ssm_im=state_ssm_im[i].reshape(nbs, -1),
                    ret=state_ret[i], conv=state_conv[i])
        h_s, st_s = _decode_layer(h_s, p_sample[i].reshape(rows_s, -1), w, consts_s, past,
                                  layer=i, nb=nbs, ntok=ntok)
        outs_p.append(st_p)
        outs_s.append(st_s)

    st_shape = lambda b: (depth, b, wa // S5_CH, S5_STATE)
    stack = lambda outs: [jnp.stack(x, axis=0) for x in zip(*outs)]
    k_p, v_p = [t.reshape(depth, nb, nh_s, HEAD_DIM, seq).transpose(0, 1, 4, 2, 3) for t in kv_p]
    sre_p, sim_p, ret_p, conv_p = stack(outs_p)
    k_s, v_s, sre_s, sim_s, ret_s, conv_s = stack(outs_s)
    kv_shape_s = (depth, nbs, ntok, nh_s, HEAD_DIM)
    return (h_p.reshape(nb, seq, d), h_s.reshape(nbs, ntok, d),
            k_p, v_p, sre_p.reshape(st_shape(nb)), sim_p.reshape(st_shape(nb)), ret_p, conv_p,
            k_s.reshape(kv_shape_s), v_s.reshape(kv_shape_s), sre_s.reshape(st_shape(nbs)),
            sim_s.reshape(st_shape(nbs)), ret_s, conv_s)
```
